```python
import jax, jax.numpy as jnp
from jax import lax
import numpy as np

D_MODEL = 1024
BATCH = 2
SEQ = 8192
DEPTH = 1

GLA_HEADS = 4
GLA_DK = 128
GLA_DV = 256
GLA_KW = GLA_HEADS * GLA_DK
GLA_VW = GLA_HEADS * GLA_DV
GATE_RANK = 16
GATE_NORMALIZER = 16.0
GLA_CHUNK = 64
CONV_WIDTH = 1024
CONV_GROUPS = 8
CONV_K = 3
PLE_DIM = 256
EPS = 1e-6

SPLITS = [GLA_KW, GLA_KW, GLA_VW, GLA_VW, GATE_RANK,
          CONV_WIDTH, CONV_WIDTH, CONV_WIDTH, CONV_WIDTH,
          D_MODEL, D_MODEL]
IN_COLS = sum(SPLITS)

kernel_name = "hybrid_gla_shortconv_gated_merge"


def rms_norm(x, g):
    xf = x.astype(jnp.float32)
    y = xf * lax.rsqrt(jnp.mean(xf * xf, axis=-1, keepdims=True) + EPS)
    return (y * g.astype(jnp.float32)).astype(x.dtype)


def gla_chunked(q, k, v, gk):
    bsz, s, h, dk = q.shape
    dv = v.shape[-1]
    n = s // GLA_CHUNK

    def to_chunks(t):
        return t.astype(jnp.float32).reshape(bsz, n, GLA_CHUNK, h, t.shape[-1]).transpose(1, 0, 3, 2, 4)

    q, k, v, gk = to_chunks(q) * (dk ** -0.5), to_chunks(k), to_chunks(v), to_chunks(gk)
    b = jnp.cumsum(gk, axis=3)
    b_last = b[:, :, :, -1:, :]
    q_in = q * jnp.exp(b)
    k_in = k * jnp.exp(-b)
    k_dec = k * jnp.exp(b_last - b)
    causal = jnp.tril(jnp.ones((GLA_CHUNK, GLA_CHUNK), dtype=bool))
    a = jnp.where(causal, jnp.einsum('nbhid,nbhjd->nbhij', q_in, k_in), 0.0)
    o_intra = jnp.einsum('nbhij,nbhjv->nbhiv', a, v)
    decay = jnp.exp(b_last[:, :, :, 0, :])

    def step(state, xs):
        q_n, k_n, v_n, d_n = xs
        o_n = jnp.einsum('bhcd,bhdv->bhcv', q_n, state)
        state = d_n[..., None] * state + jnp.einsum('bhcd,bhcv->bhdv', k_n, v_n)
        return state, o_n

    state0 = jnp.zeros((bsz, h, dk, dv), jnp.float32)
    _, o_inter = lax.scan(step, state0, (q_in, k_dec, v, decay))
    o = o_intra + o_inter
    return o.transpose(1, 0, 3, 2, 4).reshape(bsz, s, h, dv)


def causal_dwconv(u, w):
    s = u.shape[1]
    u_pad = jnp.pad(u, ((0, 0), (CONV_K - 1, 0), (0, 0)))
    y = w[0] * u_pad[:, 0:s, :]
    for j in range(1, CONV_K):
        y = y + w[j] * u_pad[:, j:j + s, :]
    return y


def setup_inputs(seed: int = 0) -> dict:
    key = jax.random.key(seed)
    ks = jax.random.split(key, 20)
    nrm = lambda k, shape, scale: jax.random.normal(k, shape, jnp.float32) * scale
    gain = lambda k, shape: 1.0 + 0.05 * jax.random.normal(k, shape, jnp.float32)
    return {
        "x": nrm(ks[0], (BATCH, SEQ, D_MODEL), 1.0),
        "p": nrm(ks[1], (DEPTH, BATCH, SEQ, PLE_DIM), 1.0),
        "norm_mix_g": gain(ks[2], (DEPTH, D_MODEL)),
        "w_in": nrm(ks[3], (DEPTH, D_MODEL, IN_COLS), D_MODEL ** -0.5),
        "b_merge": nrm(ks[4], (DEPTH, 2 * D_MODEL), 0.02),
        "w_gk2": nrm(ks[5], (DEPTH, GATE_RANK, GLA_KW), GATE_RANK ** -0.5),
        "b_gk": nrm(ks[6], (DEPTH, GLA_KW), 0.02),
        "gla_norm_g": gain(ks[7], (DEPTH, GLA_DV)),
        "conv_w": nrm(ks[8], (DEPTH, CONV_K, CONV_WIDTH), CONV_K ** -0.5),
        "w_branch_a": nrm(ks[9], (DEPTH, GLA_VW, D_MODEL), GLA_VW ** -0.5),
        "w_branch_c": nrm(ks[10], (DEPTH, CONV_WIDTH, D_MODEL), CONV_WIDTH ** -0.5),
        "w_out": nrm(ks[11], (DEPTH, D_MODEL, D_MODEL), D_MODEL ** -0.5),
        "norm_ple_g": gain(ks[12], (DEPTH, D_MODEL)),
        "w_ple_gate": nrm(ks[13], (DEPTH, D_MODEL, D_MODEL), D_MODEL ** -0.5),
        "w_ple_proj": nrm(ks[14], (DEPTH, PLE_DIM, D_MODEL), PLE_DIM ** -0.5),
        "norm_final_g": gain(ks[15], (D_MODEL,)),
    }


def reference(x, p, norm_mix_g, w_in, b_merge, w_gk2, b_gk, gla_norm_g, conv_w,
              w_branch_a, w_branch_c, w_out, norm_ple_g, w_ple_gate, w_ple_proj,
              norm_final_g):
    bsz, s, _ = x.shape
    idx = np.cumsum(SPLITS)[:-1].tolist()
    for i in range(DEPTH):
        h = rms_norm(x, norm_mix_g[i])
        proj = jnp.einsum('bsd,dc->bsc', h, w_in[i])
        (q, k, v, z_a, gk_low, cb, cc, xc, z_c, g_a, g_c) = jnp.split(proj, idx, axis=-1)

        gk = jax.nn.log_sigmoid((jnp.einsum('bsr,rk->bsk', gk_low, w_gk2[i]) + b_gk[i]).astype(jnp.float32)) / GATE_NORMALIZER
        o = gla_chunked(q.reshape(bsz, s, GLA_HEADS, GLA_DK),
                        k.reshape(bsz, s, GLA_HEADS, GLA_DK),
                        v.reshape(bsz, s, GLA_HEADS, GLA_DV),
                        gk.reshape(bsz, s, GLA_HEADS, GLA_DK))
        o = rms_norm(o, gla_norm_g[i]).reshape(bsz, s, GLA_VW).astype(x.dtype)
        y_a = jnp.einsum('bsv,vd->bsd', o * jax.nn.silu(z_a), w_branch_a[i])

        u = causal_dwconv(cc * xc, conv_w[i])
        y_c = jnp.einsum('bsc,cd->bsd', (cb * u) * jax.nn.silu(z_c), w_branch_c[i])

        b_ga, b_gc = b_merge[i][:D_MODEL], b_merge[i][D_MODEL:]
        merged = jax.nn.sigmoid(g_a + b_ga) * y_a + jax.nn.sigmoid(g_c + b_gc) * y_c
        x = x + jnp.einsum('bsd,de->bse', merged, w_out[i])

        ple_gate = jax.nn.sigmoid(jnp.einsum('bsd,de->bse', rms_norm(x, norm_ple_g[i]), w_ple_gate[i]))
        x = x + ple_gate * jnp.einsum('bsq,qd->bsd', p[i], w_ple_proj[i])
    return rms_norm(x, norm_final_g)
```

```python
import functools

import jax
import jax.numpy as jnp
from jax import lax
from jax.experimental import pallas as pl
from jax.experimental.pallas import tpu as pltpu

D_MODEL = 1024
GLA_HEADS = 4
GLA_DK = 128
GLA_DV = 256
GLA_KW = GLA_HEADS * GLA_DK
GLA_VW = GLA_HEADS * GLA_DV
GATE_RANK = 16
GATE_NORMALIZER = 16.0
GLA_CHUNK = 64
CONV_WIDTH = 1024
CONV_K = 3
PLE_DIM = 256
EPS = 1e-6

LANE = 128
SUBLANE = 8
SEQ_TILE = 256
VMEM_LIMIT_BYTES = 56 * 1024 * 1024

OFF_Q = 0
OFF_K = OFF_Q + GLA_KW
OFF_V = OFF_K + GLA_KW
OFF_ZA = OFF_V + GLA_VW
OFF_GKL = OFF_ZA + GLA_VW
OFF_CB = OFF_GKL + LANE
OFF_CC = OFF_CB + CONV_WIDTH
OFF_XC = OFF_CC + CONV_WIDTH
OFF_ZC = OFF_XC + CONV_WIDTH
OFF_GA = OFF_ZC + CONV_WIDTH
OFF_GC = OFF_GA + D_MODEL
IN_COLS_ALIGNED = OFF_GC + D_MODEL

_NT = (((1,), (1,)), ((), ()))
_TN = (((0,), (0,)), ((), ()))


def _bf16(a):
    return a.astype(jnp.bfloat16)


def _dot(a, b):
    return jnp.dot(a, b, preferred_element_type=jnp.float32)


def _rms(xf, g):
    ms = jnp.mean(xf * xf, axis=-1, keepdims=True)
    return xf * lax.rsqrt(ms + EPS) * g


def _sigmoid(a):
    return 0.5 * jnp.tanh(0.5 * a) + 0.5


def _log_sigmoid(a):
    return jnp.minimum(a, 0.0) - jnp.log(1.0 + jnp.exp(-jnp.abs(a)))


def _layer_kernel(x_ref, p_ref, w_in_ref, w_gk2_ref, w_a_ref, w_c_ref, w_out_ref,
                  w_pg_ref, w_pe_ref, g_mix_ref, b_gk_ref, g_gla_ref, conv_w_ref,
                  b_merge_ref, g_ple_ref, g_fin_ref, out_ref,
                  state_ref, carry_ref, o_ref, *, apply_final_norm):
    ts = x_ref.shape[1]
    n_chunks = ts // GLA_CHUNK

    @pl.when(pl.program_id(1) == 0)
    def _():
        state_ref[...] = jnp.zeros_like(state_ref)
        carry_ref[...] = jnp.zeros_like(carry_ref)

    x = x_ref[0]
    h = _bf16(_rms(x, g_mix_ref[...]))

    def proj(off, width):
        return _dot(h, w_in_ref[:, off:off + width])

    q = proj(OFF_Q, GLA_KW)
    k = proj(OFF_K, GLA_KW)
    v = _bf16(proj(OFF_V, GLA_VW))
    gk_low = _bf16(proj(OFF_GKL, LANE))
    pre = _dot(gk_low, w_gk2_ref[...]) + b_gk_ref[...]
    gk = _log_sigmoid(pre) * (1.0 / GATE_NORMALIZER)

    row = lax.broadcasted_iota(jnp.int32, (ts, ts), 0)
    col = lax.broadcasted_iota(jnp.int32, (ts, ts), 1)
    same_chunk = (row // GLA_CHUNK) == (col // GLA_CHUNK)
    tri = jnp.where(same_chunk & (col <= row), 1.0, 0.0).astype(jnp.bfloat16)
    gk_hi = _bf16(gk)
    gk_lo = _bf16(gk - gk_hi.astype(jnp.float32))
    b = _dot(tri, gk_hi) + _dot(tri, gk_lo)
    b_last = b.reshape(n_chunks, GLA_CHUNK, GLA_KW)[:, GLA_CHUNK - 1:GLA_CHUNK, :]
    b_last_full = jnp.broadcast_to(b_last, (n_chunks, GLA_CHUNK, GLA_KW)).reshape(ts, GLA_KW)

    q_in = _bf16(q * (jnp.exp(b) * (GLA_DK ** -0.5)))
    k_in = _bf16(k * jnp.exp(-b))
    k_dec = _bf16(k * jnp.exp(b_last_full - b))
    decay = jnp.exp(b_last.reshape(n_chunks, GLA_KW))

    ci = lax.broadcasted_iota(jnp.int32, (GLA_CHUNK, GLA_CHUNK), 0)
    cj = lax.broadcasted_iota(jnp.int32, (GLA_CHUNK, GLA_CHUNK), 1)
    causal = cj <= ci

    for hd in range(GLA_HEADS):
        ks = slice(hd * GLA_DK, (hd + 1) * GLA_DK)
        vs = slice(hd * GLA_DV, (hd + 1) * GLA_DV)
        s_t = state_ref[hd]
        for c in range(n_chunks):
            rs = slice(c * GLA_CHUNK, (c + 1) * GLA_CHUNK)
            qc = q_in[rs, ks]
            vc = v[rs, vs]
            a = lax.dot_general(qc, k_in[rs, ks], _NT, preferred_element_type=jnp.float32)
            a = _bf16(jnp.where(causal, a, 0.0))
            o = _dot(a, vc) + lax.dot_general(qc, _bf16(s_t), _NT,
                                              preferred_element_type=jnp.float32)
            o_ref[rs, vs] = o
            upd = lax.dot_general(vc, k_dec[rs, ks], _TN, preferred_element_type=jnp.float32)
            s_t = s_t * decay[c:c + 1, ks] + upd
        state_ref[hd] = s_t

    z_a = proj(OFF_ZA, GLA_VW)
    g_gla = g_gla_ref[...]
    gated_parts = []
    for hd in range(GLA_HEADS):
        vs = slice(hd * GLA_DV, (hd + 1) * GLA_DV)
        z_h = z_a[:, vs]
        gated_parts.append(_bf16(_rms(o_ref[:, vs], g_gla) * (z_h * _sigmoid(z_h))))
    y_a = _dot(jnp.concatenate(gated_parts, axis=1), w_a_ref[...])

    m = proj(OFF_CC, CONV_WIDTH) * proj(OFF_XC, CONV_WIDTH)
    ext = jnp.concatenate([carry_ref[...], m], axis=0)
    carry_ref[...] = m[ts - SUBLANE:, :]
    m1 = pltpu.roll(ext, 1, 0)[SUBLANE:, :]
    m2 = pltpu.roll(ext, 2, 0)[SUBLANE:, :]
    cw = conv_w_ref[...]
    u = cw[0:1, :] * m2 + cw[1:2, :] * m1 + cw[2:3, :] * m
    z_c = proj(OFF_ZC, CONV_WIDTH)
    gated_c = _bf16((proj(OFF_CB, CONV_WIDTH) * u) * (z_c * _sigmoid(z_c)))
    y_c = _dot(gated_c, w_c_ref[...])

    b_merge = b_merge_ref[...]
    g_a = proj(OFF_GA, D_MODEL) + b_merge[:, :D_MODEL]
    g_c = proj(OFF_GC, D_MODEL) + b_merge[:, D_MODEL:]
    merged = _bf16(_sigmoid(g_a) * y_a + _sigmoid(g_c) * y_c)
    x1 = x + _dot(merged, w_out_ref[...])

    hn = _bf16(_rms(x1, g_ple_ref[...]))
    ple_gate = _sigmoid(_dot(hn, w_pg_ref[...]))
    x2 = x1 + ple_gate * _dot(_bf16(p_ref[0]), w_pe_ref[...])

    if apply_final_norm:
        x2 = _rms(x2, g_fin_ref[...])
    out_ref[0] = x2


def _const_spec(shape):
    zeros = (0,) * len(shape)
    return pl.BlockSpec(shape, lambda b, t: zeros, pipeline_mode=pl.Buffered(1))


def _layer(x, p_i, weights, vectors, *, apply_final_norm):
    bsz, seq, d = x.shape
    assert d == D_MODEL and seq % SEQ_TILE == 0 and SEQ_TILE % GLA_CHUNK == 0
    tile = lambda width: pl.BlockSpec((1, SEQ_TILE, width), lambda b, t: (b, t, 0))
    in_specs = ([tile(D_MODEL), tile(PLE_DIM)]
                + [_const_spec(w.shape) for w in weights]
                + [_const_spec(vv.shape) for vv in vectors])
    return pl.pallas_call(
        functools.partial(_layer_kernel, apply_final_norm=apply_final_norm),
        grid=(bsz, seq // SEQ_TILE),
        in_specs=in_specs,
        out_specs=tile(D_MODEL),
        out_shape=jax.ShapeDtypeStruct(x.shape, jnp.float32),
        scratch_shapes=[
            pltpu.VMEM((GLA_HEADS, GLA_DV, GLA_DK), jnp.float32),
            pltpu.VMEM((SUBLANE, CONV_WIDTH), jnp.float32),
            pltpu.VMEM((SEQ_TILE, GLA_VW), jnp.float32),
        ],
        compiler_params=pltpu.CompilerParams(
            dimension_semantics=("arbitrary", "arbitrary"),
            vmem_limit_bytes=VMEM_LIMIT_BYTES),
        name="hybrid_layer",
    )(x, p_i, *weights, *vectors)


def _aligned_w_in(w):
    gate_end = OFF_GKL + GATE_RANK
    pad = jnp.zeros((w.shape[0], LANE - GATE_RANK), w.dtype)
    return _bf16(jnp.concatenate([w[:, :gate_end], pad, w[:, gate_end:]], axis=1))


def kernel(x, p, norm_mix_g, w_in, b_merge, w_gk2, b_gk, gla_norm_g, conv_w, w_branch_a,
           w_branch_c, w_out, norm_ple_g, w_ple_gate, w_ple_proj, norm_final_g):
    depth = w_in.shape[0]
    row = lambda a: a.reshape(1, -1).astype(jnp.float32)
    for i in range(depth):
        w_gk2_pad = jnp.pad(w_gk2[i], ((0, LANE - GATE_RANK), (0, 0)))
        weights = [_aligned_w_in(w_in[i]), _bf16(w_gk2_pad), _bf16(w_branch_a[i]),
                   _bf16(w_branch_c[i]), _bf16(w_out[i]), _bf16(w_ple_gate[i]),
                   _bf16(w_ple_proj[i])]
        vectors = [row(norm_mix_g[i]), row(b_gk[i]), row(gla_norm_g[i]),
                   conv_w[i].astype(jnp.float32), row(b_merge[i]), row(norm_ple_g[i]),
                   row(norm_final_g)]
        x = _layer(x, p[i], weights, vectors, apply_final_norm=(i == depth - 1))
    return x
```

```python
import functools

import jax
import jax.numpy as jnp
from jax import lax
from jax.experimental import pallas as pl
from jax.experimental.pallas import tpu as pltpu

D_MODEL = 1024
GLA_HEADS = 4
GLA_DK = 128
GLA_DV = 256
GLA_KW = GLA_HEADS * GLA_DK
GLA_VW = GLA_HEADS * GLA_DV
GATE_RANK = 16
GATE_NORMALIZER = 16.0
GLA_CHUNK = 64
CONV_WIDTH = 1024
CONV_K = 3
PLE_DIM = 256
EPS = 1e-6

LANE = 128
SUBLANE = 8
SEQ_TILE = 256
VMEM_LIMIT_BYTES = 58 * 1024 * 1024

OFF_Q = 0
OFF_K = OFF_Q + GLA_KW
OFF_V = OFF_K + GLA_KW
OFF_ZA = OFF_V + GLA_VW
COLS_A = OFF_ZA + GLA_VW
OFF_CB = 0
OFF_CC = OFF_CB + CONV_WIDTH
OFF_XC = OFF_CC + CONV_WIDTH
OFF_ZC = OFF_XC + CONV_WIDTH
OFF_GA = OFF_ZC + CONV_WIDTH
OFF_GC = OFF_GA + D_MODEL
COLS_B = OFF_GC + D_MODEL

_NT = (((1,), (1,)), ((), ()))
_TN = (((0,), (0,)), ((), ()))
F32 = jnp.float32


def _bf16(a):
    return a.astype(jnp.bfloat16)


def _rms(xf, g):
    ms = jnp.mean(xf * xf, axis=-1, keepdims=True)
    return xf * lax.rsqrt(ms + EPS) * g


def _sigmoid(a):
    return 0.5 * jnp.tanh(0.5 * a) + 0.5


def _log_sigmoid(a):
    return jnp.minimum(a, 0.0) - jnp.log(1.0 + jnp.exp(-jnp.abs(a)))


def _round_up(n, m):
    return (n + m - 1) // m * m


def _tile_stages(s, x_ref, p_ref, w_a_in_ref, w_gkl_ref, w_b_in_ref, w_gk2_ref, w_a_ref,
                 w_c_ref, w_out_ref, w_pg_ref, w_pe_ref, g_mix_ref, b_gk_ref, g_gla_ref,
                 conv_w_ref, b_merge_ref, g_ple_ref, g_fin_ref, out_ref, state_ref, carry_ref,
                 apply_final_norm):
    ts = x_ref.shape[1]
    n_chunks = ts // GLA_CHUNK

    x = x_ref[s]
    h = _bf16(_rms(x, g_mix_ref[...]))
    yield

    def proj_a(off, width):
        return jnp.dot(h, w_a_in_ref[:, off:off + width], preferred_element_type=F32)

    def proj_b(off, width):
        return jnp.dot(h, w_b_in_ref[:, off:off + width], preferred_element_type=F32)

    gk_low = _bf16(jnp.dot(h, w_gkl_ref[...], preferred_element_type=F32))
    q = proj_a(OFF_Q, GLA_KW)
    k = proj_a(OFF_K, GLA_KW)
    yield
    pre = jnp.dot(gk_low, w_gk2_ref[...], preferred_element_type=F32) + b_gk_ref[...]
    v = _bf16(proj_a(OFF_V, GLA_VW))
    yield
    gk = _log_sigmoid(pre) * (1.0 / GATE_NORMALIZER)
    gk_hi = _bf16(gk)
    gk_lo = _bf16(gk - gk_hi.astype(F32))
    z_a = proj_a(OFF_ZA, GLA_VW)
    yield
    row = lax.broadcasted_iota(jnp.int32, (ts, ts), 0)
    col = lax.broadcasted_iota(jnp.int32, (ts, ts), 1)
    causal = col <= row
    tri = jnp.where(causal, 1.0, 0.0).astype(jnp.bfloat16)
    cum = (jnp.dot(tri, gk_hi, preferred_element_type=F32)
           + jnp.dot(tri, gk_lo, preferred_element_type=F32))
    cc = proj_b(OFF_CC, CONV_WIDTH)
    yield
    cum_end = cum[ts - 1:ts, :]
    starts = [jnp.zeros((1, GLA_KW), F32)]
    starts += [cum[c * GLA_CHUNK - 1:c * GLA_CHUNK, :] for c in range(1, n_chunks)]
    start_full = jnp.concatenate(
        [jnp.broadcast_to(st, (GLA_CHUNK, GLA_KW)) for st in starts], axis=0)
    exp_start_full = jnp.concatenate(
        [jnp.broadcast_to(jnp.exp(st), (GLA_CHUNK, GLA_KW)) for st in starts], axis=0)
    q_loc = q * (jnp.exp(cum - start_full) * (GLA_DK ** -0.5))
    q_in = _bf16(q_loc)
    q_tile = _bf16(q_loc * exp_start_full)
    k_end = _bf16(k * jnp.exp(cum_end - cum))
    k_rel = []
    for c in range(n_chunks):
        n = _round_up((c + 1) * GLA_CHUNK, LANE)
        k_rel.append(_bf16(k[:n, :] * jnp.exp(starts[c] - cum[:n, :])))
    tile_decay = jnp.exp(cum_end)
    m = cc * proj_b(OFF_XC, CONV_WIDTH)
    yield

    scores, o_state = [], []
    for hd in range(GLA_HEADS):
        ks = slice(hd * GLA_DK, (hd + 1) * GLA_DK)
        vs = slice(hd * GLA_DV, (hd + 1) * GLA_DV)
        s_t = state_ref[s, hd]
        blocks = []
        for c in range(n_chunks):
            rs = slice(c * GLA_CHUNK, (c + 1) * GLA_CHUNK)
            a_c = lax.dot_general(q_in[rs, ks], k_rel[c][:, ks], _NT,
                                  preferred_element_type=F32)
            if a_c.shape[1] < ts:
                a_c = jnp.concatenate(
                    [a_c, jnp.zeros((GLA_CHUNK, ts - a_c.shape[1]), F32)], axis=1)
            blocks.append(a_c)
        scores.append(jnp.concatenate(blocks, axis=0))
        o_state.append(lax.dot_general(q_tile[:, ks], _bf16(s_t), _NT,
                                       preferred_element_type=F32))
        state_ref[s, hd] = s_t * tile_decay[:, ks] + lax.dot_general(
            v[:, vs], k_end[:, ks], _TN, preferred_element_type=F32)
    cb = proj_b(OFF_CB, CONV_WIDTH)
    yield

    ext = jnp.concatenate([carry_ref[s], m], axis=0)
    carry_ref[s] = m[ts - SUBLANE:, :]
    m1 = pltpu.roll(ext, 1, 0)[SUBLANE:, :]
    m2 = pltpu.roll(ext, 2, 0)[SUBLANE:, :]
    cw = conv_w_ref[...]
    u = cb * (cw[0:1, :] * m2 + cw[1:2, :] * m1 + cw[2:3, :] * m)
    attn = [_bf16(jnp.where(causal, sc, 0.0)) for sc in scores]
    z_c = proj_b(OFF_ZC, CONV_WIDTH)
    yield
    outs = []
    for hd in range(GLA_HEADS):
        vs = slice(hd * GLA_DV, (hd + 1) * GLA_DV)
        outs.append(o_state[hd] + jnp.dot(attn[hd], v[:, vs], preferred_element_type=F32))
    gated_c = _bf16(u * (z_c * _sigmoid(z_c)))
    y_c = jnp.dot(gated_c, w_c_ref[...], preferred_element_type=F32)
    yield
    g_gla = g_gla_ref[...]
    gated = []
    for hd in range(GLA_HEADS):
        z_h = z_a[:, hd * GLA_DV:(hd + 1) * GLA_DV]
        gated.append(_bf16(_rms(outs[hd], g_gla) * (z_h * _sigmoid(z_h))))
    b_merge = b_merge_ref[...]
    g_c = proj_b(OFF_GC, D_MODEL) + b_merge[:, D_MODEL:]
    yield
    y_a = jnp.dot(jnp.concatenate(gated, axis=1), w_a_ref[...], preferred_element_type=F32)
    y_c = _sigmoid(g_c) * y_c
    g_a = proj_b(OFF_GA, D_MODEL) + b_merge[:, :D_MODEL]
    yield
    pe = jnp.dot(_bf16(p_ref[s]), w_pe_ref[...], preferred_element_type=F32)
    yield

    merged = _bf16(_sigmoid(g_a) * y_a + y_c)
    yield
    x1 = x + jnp.dot(merged, w_out_ref[...], preferred_element_type=F32)
    yield

    hn = _bf16(_rms(x1, g_ple_ref[...]))
    yield
    ple_gate = _sigmoid(jnp.dot(hn, w_pg_ref[...], preferred_element_type=F32))
    yield
    x2 = x1 + ple_gate * pe
    if apply_final_norm:
        x2 = _rms(x2, g_fin_ref[...])
    out_ref[s] = x2


def _layer_kernel(*refs, apply_final_norm, lag):
    x_ref = refs[0]
    state_ref, carry_ref = refs[-2], refs[-1]

    @pl.when(pl.program_id(1) == 0)
    def _():
        state_ref[...] = jnp.zeros_like(state_ref)
        carry_ref[...] = jnp.zeros_like(carry_ref)

    streams = [_tile_stages(s, *refs, apply_final_norm) for s in range(x_ref.shape[0])]
    started, live, step = 0, [], 0
    while started < len(streams) or live:
        while started < len(streams) and step >= started * lag:
            live.append(streams[started])
            started += 1
        for g in list(live):
            try:
                next(g)
            except StopIteration:
                live.remove(g)
        step += 1


def _const_spec(shape):
    zeros = (0,) * len(shape)
    return pl.BlockSpec(shape, lambda b, t: zeros, pipeline_mode=pl.Buffered(1))


def _layer(x, p_i, weights, vectors, *, apply_final_norm):
    bsz, seq, d = x.shape
    assert d == D_MODEL and seq % SEQ_TILE == 0 and SEQ_TILE % LANE == 0
    nb = 2 if bsz % 2 == 0 else 1
    tile = lambda width: pl.BlockSpec((nb, SEQ_TILE, width), lambda b, t: (b, t, 0))
    in_specs = ([tile(D_MODEL), tile(PLE_DIM)]
                + [_const_spec(w.shape) for w in weights]
                + [_const_spec(vv.shape) for vv in vectors])
    return pl.pallas_call(
        functools.partial(_layer_kernel, apply_final_norm=apply_final_norm, lag=0),
        grid=(bsz // nb, seq // SEQ_TILE),
        in_specs=in_specs,
        out_specs=tile(D_MODEL),
        out_shape=jax.ShapeDtypeStruct(x.shape, jnp.float32),
        scratch_shapes=[
            pltpu.VMEM((nb, GLA_HEADS, GLA_DV, GLA_DK), jnp.float32),
            pltpu.VMEM((nb, SUBLANE, CONV_WIDTH), jnp.float32),
        ],
        compiler_params=pltpu.CompilerParams(
            dimension_semantics=("arbitrary", "arbitrary"),
            vmem_limit_bytes=VMEM_LIMIT_BYTES),
        name="hybrid_layer",
    )(x, p_i, *weights, *vectors)


def kernel(x, p, norm_mix_g, w_in, b_merge, w_gk2, b_gk, gla_norm_g, conv_w, w_branch_a,
           w_branch_c, w_out, norm_ple_g, w_ple_gate, w_ple_proj, norm_final_g):
    depth = w_in.shape[0]
    row = lambda a: a.reshape(1, -1).astype(jnp.float32)
    gate_lo, gate_hi = COLS_A, COLS_A + GATE_RANK
    for i in range(depth):
        w_gkl = jnp.pad(w_in[i][:, gate_lo:gate_hi], ((0, 0), (0, LANE - GATE_RANK)))
        w_gk2_pad = jnp.pad(w_gk2[i], ((0, LANE - GATE_RANK), (0, 0)))
        weights = [_bf16(w_in[i][:, :gate_lo]), _bf16(w_gkl), _bf16(w_in[i][:, gate_hi:]),
                   _bf16(w_gk2_pad), _bf16(w_branch_a[i]), _bf16(w_branch_c[i]),
                   _bf16(w_out[i]), _bf16(w_ple_gate[i]), _bf16(w_ple_proj[i])]
        vectors = [row(norm_mix_g[i]), row(b_gk[i]), row(gla_norm_g[i]),
                   conv_w[i].astype(jnp.float32), row(b_merge[i]), row(norm_ple_g[i]),
                   row(norm_final_g)]
        x = _layer(x, p[i], weights, vectors, apply_final_norm=(i == depth - 1))
    return x
```

```python
import functools

import jax
import jax.numpy as jnp
from jax import lax
from jax.experimental import pallas as pl
from jax.experimental.pallas import tpu as pltpu

D_MODEL = 1024
GLA_HEADS = 4
GLA_DK = 128
GLA_DV = 256
GLA_KW = GLA_HEADS * GLA_DK
GLA_VW = GLA_HEADS * GLA_DV
GATE_RANK = 16
GATE_NORMALIZER = 16.0
GLA_CHUNK = 64
CONV_WIDTH = 1024
CONV_K = 3
PLE_DIM = 256
EPS = 1e-6

LANE = 128
SUBLANE = 8
SEQ_TILE = 256
PREP_STEPS = 8
VMEM_LIMIT_BYTES = 60 * 1024 * 1024

OFF_Q = 0
OFF_K = OFF_Q + GLA_KW
OFF_V = OFF_K + GLA_KW
OFF_ZA = OFF_V + GLA_VW
COLS_A = OFF_ZA + GLA_VW
OFF_CB = 0
OFF_CC = OFF_CB + CONV_WIDTH
OFF_XC = OFF_CC + CONV_WIDTH
OFF_ZC = OFF_XC + CONV_WIDTH
OFF_GA = OFF_ZC + CONV_WIDTH
OFF_GC = OFF_GA + D_MODEL
COLS_B = OFF_GC + D_MODEL
IN_COLS = COLS_A + GATE_RANK + COLS_B

_NT = (((1,), (1,)), ((), ()))
_TN = (((0,), (0,)), ((), ()))
F32 = jnp.float32


def _bf16(a):
    return a.astype(jnp.bfloat16)


def _rms(xf, g):
    ms = jnp.mean(xf * xf, axis=-1, keepdims=True)
    return xf * lax.rsqrt(ms + EPS) * g


def _sigmoid(a):
    return 0.5 * jnp.tanh(0.5 * a) + 0.5


def _log_sigmoid(a):
    return jnp.minimum(a, 0.0) - jnp.log(1.0 + jnp.exp(-jnp.abs(a)))


def _round_up(n, m):
    return (n + m - 1) // m * m


def _prep_kernel(w_in_ref, w_gk2_ref, wa_ref, wc_ref, wo_ref, wpg_ref, wpe_ref,
                 a_out, gkl_out, b_out, gk2_out, wa_out, wc_out, wo_out, wpg_out, wpe_out):
    a_out[...] = _bf16(w_in_ref[:, :COLS_A])
    gate = w_in_ref[:, COLS_A:COLS_A + LANE]
    lane = lax.broadcasted_iota(jnp.int32, gate.shape, 1)
    gkl_out[...] = _bf16(jnp.where(lane < GATE_RANK, gate, 0.0))
    b_out[...] = _bf16(w_in_ref[:, COLS_A + GATE_RANK:])
    wa_out[...] = _bf16(wa_ref[...])
    wc_out[...] = _bf16(wc_ref[...])
    wo_out[...] = _bf16(wo_ref[...])
    wpg_out[...] = _bf16(wpg_ref[...])
    wpe_out[...] = _bf16(wpe_ref[...])

    @pl.when(pl.program_id(0) == 0)
    def _():
        gk2_out[...] = jnp.zeros_like(gk2_out)
        gk2_out[:GATE_RANK, :] = _bf16(w_gk2_ref[...])


def _prep_weights(layer, w_in, w_gk2, w_branch_a, w_branch_c, w_out, w_ple_gate, w_ple_proj):
    assert w_in.shape[1:] == (D_MODEL, IN_COLS)

    def rows(a):
        r, c = a.shape[1] // PREP_STEPS, a.shape[2]
        return pl.BlockSpec((None, r, c), lambda i: (layer, i, 0))

    def out_rows(r, c):
        return pl.BlockSpec((r // PREP_STEPS, c), lambda i: (i, 0))

    whole_gk2_in = pl.BlockSpec((None, GATE_RANK, GLA_KW), lambda i: (layer, 0, 0))
    whole_gk2_out = pl.BlockSpec((LANE, GLA_KW), lambda i: (0, 0))
    shapes = [(D_MODEL, COLS_A), (D_MODEL, LANE), (D_MODEL, COLS_B), (LANE, GLA_KW),
              (GLA_VW, D_MODEL), (CONV_WIDTH, D_MODEL), (D_MODEL, D_MODEL), (D_MODEL, D_MODEL),
              (PLE_DIM, D_MODEL)]
    out_specs = [out_rows(*sh) for sh in shapes]
    out_specs[3] = whole_gk2_out
    return pl.pallas_call(
        _prep_kernel,
        grid=(PREP_STEPS,),
        in_specs=[rows(w_in), whole_gk2_in, rows(w_branch_a), rows(w_branch_c), rows(w_out),
                  rows(w_ple_gate), rows(w_ple_proj)],
        out_specs=out_specs,
        out_shape=[jax.ShapeDtypeStruct(sh, jnp.bfloat16) for sh in shapes],
        compiler_params=pltpu.CompilerParams(dimension_semantics=("arbitrary",)),
        name="prep_weights",
    )(w_in, w_gk2, w_branch_a, w_branch_c, w_out, w_ple_gate, w_ple_proj)


def _front_stages(s, x_ref, w_a_in_ref, w_gkl_ref, w_b_in_ref, w_gk2_ref, w_a_ref, w_c_ref,
                  g_mix_ref, b_gk_ref, g_gla_ref, conv_w_ref, b_merge_ref,
                  state_ref, carry_ref, merged_ref, x_keep_ref):
    ts = x_ref.shape[1]
    n_chunks = ts // GLA_CHUNK

    h = _bf16(_rms(x_ref[s], g_mix_ref[...]))
    yield

    def proj_a(off, width):
        return jnp.dot(h, w_a_in_ref[:, off:off + width], preferred_element_type=F32)

    def proj_b(off, width):
        return jnp.dot(h, w_b_in_ref[:, off:off + width], preferred_element_type=F32)

    gk_low = _bf16(jnp.dot(h, w_gkl_ref[...], preferred_element_type=F32))
    q = proj_a(OFF_Q, GLA_KW)
    k = proj_a(OFF_K, GLA_KW)
    yield
    pre = jnp.dot(gk_low, w_gk2_ref[...], preferred_element_type=F32) + b_gk_ref[...]
    v = _bf16(proj_a(OFF_V, GLA_VW))
    yield
    gk = _log_sigmoid(pre) * (1.0 / GATE_NORMALIZER)
    gk_hi = _bf16(gk)
    gk_lo = _bf16(gk - gk_hi.astype(F32))
    z_a = proj_a(OFF_ZA, GLA_VW)
    act_a = _bf16(z_a * _sigmoid(z_a))
    yield
    row = lax.broadcasted_iota(jnp.int32, (ts, ts), 0)
    col = lax.broadcasted_iota(jnp.int32, (ts, ts), 1)
    causal = col <= row
    tri = jnp.where(causal, 1.0, 0.0).astype(jnp.bfloat16)
    cum = (jnp.dot(tri, gk_hi, preferred_element_type=F32)
           + jnp.dot(tri, gk_lo, preferred_element_type=F32))
    m = proj_b(OFF_CC, CONV_WIDTH) * proj_b(OFF_XC, CONV_WIDTH)
    yield
    cum_end = cum[ts - 1:ts, :]
    starts = [jnp.zeros((1, GLA_KW), F32)]
    starts += [cum[c * GLA_CHUNK - 1:c * GLA_CHUNK, :] for c in range(1, n_chunks)]
    start_full = jnp.concatenate(
        [jnp.broadcast_to(st, (GLA_CHUNK, GLA_KW)) for st in starts], axis=0)
    exp_start_full = jnp.concatenate(
        [jnp.broadcast_to(jnp.exp(st), (GLA_CHUNK, GLA_KW)) for st in starts], axis=0)
    q_loc = q * (jnp.exp(cum - start_full) * (GLA_DK ** -0.5))
    q_in = _bf16(q_loc)
    q_tile = _bf16(q_loc * exp_start_full)
    k_end = _bf16(k * jnp.exp(cum_end - cum))
    k_rel = []
    for c in range(n_chunks):
        n = _round_up((c + 1) * GLA_CHUNK, LANE)
        k_rel.append(_bf16(k[:n, :] * jnp.exp(starts[c] - cum[:n, :])))
    tile_decay = jnp.exp(cum_end)
    ext = jnp.concatenate([carry_ref[s], m], axis=0)
    carry_ref[s] = m[ts - SUBLANE:, :]
    m1 = pltpu.roll(ext, 1, 0)[SUBLANE:, :]
    m2 = pltpu.roll(ext, 2, 0)[SUBLANE:, :]
    cw = conv_w_ref[...]
    u = proj_b(OFF_CB, CONV_WIDTH) * (cw[0:1, :] * m2 + cw[1:2, :] * m1 + cw[2:3, :] * m)
    yield

    attn, s_old = [], []
    for hd in range(GLA_HEADS):
        ks = slice(hd * GLA_DK, (hd + 1) * GLA_DK)
        vs = slice(hd * GLA_DV, (hd + 1) * GLA_DV)
        s_t = state_ref[s, hd]
        s_old.append(_bf16(s_t))
        blocks = []
        for c in range(n_chunks):
            rs = slice(c * GLA_CHUNK, (c + 1) * GLA_CHUNK)
            a_c = lax.dot_general(q_in[rs, ks], k_rel[c][:, ks], _NT,
                                  preferred_element_type=F32)
            if a_c.shape[1] < ts:
                a_c = jnp.concatenate(
                    [a_c, jnp.zeros((GLA_CHUNK, ts - a_c.shape[1]), F32)], axis=1)
            blocks.append(a_c)
        attn.append(_bf16(jnp.where(causal, jnp.concatenate(blocks, axis=0), 0.0)))
        state_ref[s, hd] = s_t * tile_decay[:, ks] + lax.dot_general(
            v[:, vs], k_end[:, ks], _TN, preferred_element_type=F32)
    z_c = proj_b(OFF_ZC, CONV_WIDTH)
    gated_c = _bf16(u * (z_c * _sigmoid(z_c)))
    yield
    g_gla = g_gla_ref[...]
    gated = []
    for hd in range(GLA_HEADS):
        ks = slice(hd * GLA_DK, (hd + 1) * GLA_DK)
        vs = slice(hd * GLA_DV, (hd + 1) * GLA_DV)
        o = (lax.dot_general(q_tile[:, ks], s_old[hd], _NT, preferred_element_type=F32)
             + jnp.dot(attn[hd], v[:, vs], preferred_element_type=F32))
        gated.append(_bf16(_rms(o, g_gla) * act_a[:, vs]))
    b_merge = b_merge_ref[...]
    gate_c = _bf16(_sigmoid(proj_b(OFF_GC, D_MODEL) + b_merge[:, D_MODEL:]))
    yield
    y_c = _bf16(gate_c * jnp.dot(gated_c, w_c_ref[...], preferred_element_type=F32))
    gate_a = _bf16(_sigmoid(proj_b(OFF_GA, D_MODEL) + b_merge[:, :D_MODEL]))
    yield
    y_a = jnp.dot(jnp.concatenate(gated, axis=1), w_a_ref[...], preferred_element_type=F32)
    merged_ref[s] = _bf16(gate_a * y_a + y_c)
    x_keep_ref[s] = x_ref[s]


def _tail_stages(s, p_ref, w_out_ref, w_pg_ref, w_pe_ref, g_ple_ref, g_fin_ref,
                 merged_ref, x_keep_ref, out_ref, apply_final_norm):
    x1 = x_keep_ref[s] + jnp.dot(merged_ref[s], w_out_ref[...], preferred_element_type=F32)
    yield
    hn = _bf16(_rms(x1, g_ple_ref[...]))
    yield
    ple_gate = _sigmoid(jnp.dot(hn, w_pg_ref[...], preferred_element_type=F32))
    x2 = x1 + ple_gate * jnp.dot(_bf16(p_ref[s]), w_pe_ref[...], preferred_element_type=F32)
    yield
    if apply_final_norm:
        x2 = _rms(x2, g_fin_ref[...])
    out_ref[s] = x2


def _run_round_robin(streams):
    live = list(streams)
    while live:
        for g in list(live):
            try:
                next(g)
            except StopIteration:
                live.remove(g)


def _layer_kernel(x_cur_ref, p_ref, w_a_in_ref, w_gkl_ref, w_b_in_ref, w_gk2_ref,
                  w_a_ref, w_c_ref, w_out_ref, w_pg_ref, w_pe_ref, g_mix_ref, b_gk_ref,
                  g_gla_ref, conv_w_ref, b_merge_ref, g_ple_ref, g_fin_ref, out_ref,
                  state_ref, carry_ref, merged_ref, x_keep_ref, *, apply_final_norm):
    n_streams = x_cur_ref.shape[0]
    t = pl.program_id(1)
    last = pl.num_programs(1) - 1

    def fronts():
        return [_front_stages(s, x_cur_ref, w_a_in_ref, w_gkl_ref, w_b_in_ref, w_gk2_ref,
                              w_a_ref, w_c_ref, g_mix_ref, b_gk_ref, g_gla_ref, conv_w_ref,
                              b_merge_ref, state_ref, carry_ref, merged_ref, x_keep_ref)
                for s in range(n_streams)]

    def tails():
        return [_tail_stages(s, p_ref, w_out_ref, w_pg_ref, w_pe_ref, g_ple_ref, g_fin_ref,
                             merged_ref, x_keep_ref, out_ref, apply_final_norm)
                for s in range(n_streams)]

    @pl.when(t == 0)
    def _():
        state_ref[...] = jnp.zeros_like(state_ref)
        carry_ref[...] = jnp.zeros_like(carry_ref)
        _run_round_robin(fronts())

    @pl.when((t > 0) & (t < last))
    def _():
        _run_round_robin(tails() + fronts())

    @pl.when(t == last)
    def _():
        _run_round_robin(tails())


def _const_spec(shape):
    zeros = (0,) * len(shape)
    return pl.BlockSpec(shape, lambda b, t: zeros, pipeline_mode=pl.Buffered(1))


def _layer(x, p_i, weights, vectors, *, apply_final_norm):
    bsz, seq, d = x.shape
    assert d == D_MODEL and seq % SEQ_TILE == 0 and SEQ_TILE % LANE == 0
    nb = 2 if bsz % 2 == 0 else 1
    n_tiles = seq // SEQ_TILE

    def tile(width, shift):
        return pl.BlockSpec(
            (nb, SEQ_TILE, width),
            lambda b, t: (b, jnp.clip(t - shift, 0, n_tiles - 1), 0))

    in_specs = ([tile(D_MODEL, 0), tile(PLE_DIM, 1)]
                + [_const_spec(w.shape) for w in weights]
                + [_const_spec(vv.shape) for vv in vectors])
    return pl.pallas_call(
        functools.partial(_layer_kernel, apply_final_norm=apply_final_norm),
        grid=(bsz // nb, n_tiles + 1),
        in_specs=in_specs,
        out_specs=tile(D_MODEL, 1),
        out_shape=jax.ShapeDtypeStruct(x.shape, jnp.float32),
        scratch_shapes=[
            pltpu.VMEM((nb, GLA_HEADS, GLA_DV, GLA_DK), jnp.float32),
            pltpu.VMEM((nb, SUBLANE, CONV_WIDTH), jnp.float32),
            pltpu.VMEM((nb, SEQ_TILE, D_MODEL), jnp.bfloat16),
            pltpu.VMEM((nb, SEQ_TILE, D_MODEL), jnp.float32),
        ],
        compiler_params=pltpu.CompilerParams(
            dimension_semantics=("arbitrary", "arbitrary"),
            vmem_limit_bytes=VMEM_LIMIT_BYTES),
        name="hybrid_layer",
    )(x, p_i, *weights, *vectors)


def kernel(x, p, norm_mix_g, w_in, b_merge, w_gk2, b_gk, gla_norm_g, conv_w, w_branch_a,
           w_branch_c, w_out, norm_ple_g, w_ple_gate, w_ple_proj, norm_final_g):
    depth = w_in.shape[0]
    row = lambda a: a.reshape(1, -1).astype(jnp.float32)
    for i in range(depth):
        weights = _prep_weights(i, w_in, w_gk2, w_branch_a, w_branch_c, w_out, w_ple_gate,
                                w_ple_proj)
        vectors = [row(norm_mix_g[i]), row(b_gk[i]), row(gla_norm_g[i]),
                   conv_w[i].astype(jnp.float32), row(b_merge[i]), row(norm_ple_g[i]),
                   row(norm_final_g)]
        x = _layer(x, p[i], weights, vectors, apply_final_norm=(i == depth - 1))
    return x
```

```python
import functools

import jax
import jax.numpy as jnp
from jax import lax
from jax.experimental import pallas as pl
from jax.experimental.pallas import tpu as pltpu

D_MODEL = 1024
GLA_HEADS = 4
GLA_DK = 128
GLA_DV = 256
GLA_KW = GLA_HEADS * GLA_DK
GLA_VW = GLA_HEADS * GLA_DV
GATE_RANK = 16
GATE_NORMALIZER = 16.0
GLA_CHUNK = 64
CONV_WIDTH = 1024
CONV_K = 3
PLE_DIM = 256
EPS = 1e-6

LANE = 128
SUBLANE = 8
SEQ_TILE = 256
PREP_STEPS = 8
VMEM_LIMIT_BYTES = 58 * 1024 * 1024

OFF_Q = 0
OFF_K = OFF_Q + GLA_KW
OFF_V = OFF_K + GLA_KW
OFF_ZA = OFF_V + GLA_VW
COLS_A = OFF_ZA + GLA_VW
OFF_CB = 0
OFF_CC = OFF_CB + CONV_WIDTH
OFF_XC = OFF_CC + CONV_WIDTH
OFF_ZC = OFF_XC + CONV_WIDTH
OFF_GA = OFF_ZC + CONV_WIDTH
OFF_GC = OFF_GA + D_MODEL
COLS_B = OFF_GC + D_MODEL
IN_COLS = COLS_A + GATE_RANK + COLS_B

_NT = (((1,), (1,)), ((), ()))
_TN = (((0,), (0,)), ((), ()))
F32 = jnp.float32


def _bf16(a):
    return a.astype(jnp.bfloat16)


def _rms(xf, g):
    ms = jnp.mean(xf * xf, axis=-1, keepdims=True)
    return xf * lax.rsqrt(ms + EPS) * g


def _sigmoid(a):
    return 0.5 * jnp.tanh(0.5 * a) + 0.5


def _log_sigmoid(a):
    return jnp.minimum(a, 0.0) - jnp.log(1.0 + jnp.exp(-jnp.abs(a)))


def _round_up(n, m):
    return (n + m - 1) // m * m


def _prep_kernel(wt_a_ref, wt_gate_ref, wt_b_ref, wt_b_next_ref, w_gk2_ref, wa_ref, wc_ref,
                 wo_ref, wpg_ref, wpe_ref,
                 a_out, gkl_out, b_out, gk2_out, wa_out, wc_out, wo_out, wpg_out, wpe_out):
    a_out[...] = _bf16(wt_a_ref[...].T)
    rows_b = jnp.concatenate([wt_b_ref[GATE_RANK:, :], wt_b_next_ref[...]], axis=0)
    b_out[...] = _bf16(rows_b.T)
    wa_out[...] = _bf16(wa_ref[...])
    wc_out[...] = _bf16(wc_ref[...])
    wo_out[...] = _bf16(wo_ref[...])
    wpg_out[...] = _bf16(wpg_ref[...])
    wpe_out[...] = _bf16(wpe_ref[...])

    @pl.when(pl.program_id(0) == 0)
    def _():
        gate_rows = jnp.concatenate(
            [wt_gate_ref[...], jnp.zeros((LANE - GATE_RANK, D_MODEL), F32)], axis=0)
        gkl_out[...] = _bf16(gate_rows.T)
        gk2_out[...] = jnp.zeros_like(gk2_out)
        gk2_out[:GATE_RANK, :] = _bf16(w_gk2_ref[...])


def _prep_weights(layer, w_in, w_gk2, w_branch_a, w_branch_c, w_out, w_ple_gate, w_ple_proj):
    assert w_in.shape[1:] == (D_MODEL, IN_COLS)
    w_in_t = jnp.swapaxes(w_in, 1, 2)
    rows_a, rows_b = COLS_A // PREP_STEPS, COLS_B // PREP_STEPS
    assert COLS_A % rows_b == 0 and COLS_A % GATE_RANK == 0 and rows_b % GATE_RANK == 0

    def rows(a):
        r, c = a.shape[1] // PREP_STEPS, a.shape[2]
        return pl.BlockSpec((None, r, c), lambda i: (layer, i, 0))

    def out_rows(r, c):
        return pl.BlockSpec((r // PREP_STEPS, c), lambda i: (i, 0))

    def out_cols(r, c):
        return pl.BlockSpec((r, c // PREP_STEPS), lambda i: (0, i))

    in_specs = [
        pl.BlockSpec((None, rows_a, D_MODEL), lambda i: (layer, i, 0)),
        pl.BlockSpec((None, GATE_RANK, D_MODEL), lambda i: (layer, COLS_A // GATE_RANK, 0)),
        pl.BlockSpec((None, rows_b, D_MODEL), lambda i: (layer, COLS_A // rows_b + i, 0)),
        pl.BlockSpec((None, GATE_RANK, D_MODEL),
                     lambda i: (layer, (COLS_A + rows_b * (i + 1)) // GATE_RANK, 0)),
        pl.BlockSpec((None, GATE_RANK, GLA_KW), lambda i: (layer, 0, 0)),
        rows(w_branch_a), rows(w_branch_c), rows(w_out), rows(w_ple_gate), rows(w_ple_proj)]
    shapes = [(D_MODEL, COLS_A), (D_MODEL, LANE), (D_MODEL, COLS_B), (LANE, GLA_KW),
              (GLA_VW, D_MODEL), (CONV_WIDTH, D_MODEL), (D_MODEL, D_MODEL), (D_MODEL, D_MODEL),
              (PLE_DIM, D_MODEL)]
    out_specs = [out_cols(*shapes[0]),
                 pl.BlockSpec(shapes[1], lambda i: (0, 0)),
                 out_cols(*shapes[2]),
                 pl.BlockSpec(shapes[3], lambda i: (0, 0))]
    out_specs += [out_rows(*sh) for sh in shapes[4:]]
    return pl.pallas_call(
        _prep_kernel,
        grid=(PREP_STEPS,),
        in_specs=in_specs,
        out_specs=out_specs,
        out_shape=[jax.ShapeDtypeStruct(sh, jnp.bfloat16) for sh in shapes],
        compiler_params=pltpu.CompilerParams(dimension_semantics=("arbitrary",)),
        name="prep_weights",
    )(w_in_t, w_in_t, w_in_t, w_in_t, w_gk2, w_branch_a, w_branch_c, w_out, w_ple_gate,
      w_ple_proj)


def _tile_stages(s, x_ref, p_ref, w_a_in_ref, w_gkl_ref, w_b_in_ref, w_gk2_ref, w_a_ref,
                 w_c_ref, w_out_ref, w_pg_ref, w_pe_ref, g_mix_ref, b_gk_ref, g_gla_ref,
                 conv_w_ref, b_merge_ref, g_ple_ref, g_fin_ref, out_ref, state_ref, carry_ref,
                 apply_final_norm):
    ts = x_ref.shape[1]
    n_chunks = ts // GLA_CHUNK

    x = x_ref[s]
    h = _bf16(_rms(x, g_mix_ref[...]))
    yield

    def proj_a(off, width):
        return jnp.dot(h, w_a_in_ref[:, off:off + width], preferred_element_type=F32)

    def proj_b(off, width):
        return jnp.dot(h, w_b_in_ref[:, off:off + width], preferred_element_type=F32)

    gk_low = _bf16(jnp.dot(h, w_gkl_ref[...], preferred_element_type=F32))
    q = proj_a(OFF_Q, GLA_KW)
    k = proj_a(OFF_K, GLA_KW)
    yield
    pre = jnp.dot(gk_low, w_gk2_ref[...], preferred_element_type=F32) + b_gk_ref[...]
    v = _bf16(proj_a(OFF_V, GLA_VW))
    yield
    gk = _log_sigmoid(pre) * (1.0 / GATE_NORMALIZER)
    gk_hi = _bf16(gk)
    gk_lo = _bf16(gk - gk_hi.astype(F32))
    z_a = proj_a(OFF_ZA, GLA_VW)
    act_a = _bf16(z_a * _sigmoid(z_a))
    yield
    row = lax.broadcasted_iota(jnp.int32, (ts, ts), 0)
    col = lax.broadcasted_iota(jnp.int32, (ts, ts), 1)
    causal = col <= row
    tri = jnp.where(causal, 1.0, 0.0).astype(jnp.bfloat16)
    cum = (jnp.dot(tri, gk_hi, preferred_element_type=F32)
           + jnp.dot(tri, gk_lo, preferred_element_type=F32))
    m = proj_b(OFF_CC, CONV_WIDTH) * proj_b(OFF_XC, CONV_WIDTH)
    yield
    cum_end = cum[ts - 1:ts, :]
    starts = [jnp.zeros((1, GLA_KW), F32)]
    starts += [cum[c * GLA_CHUNK - 1:c * GLA_CHUNK, :] for c in range(1, n_chunks)]
    start_full = jnp.concatenate(
        [jnp.broadcast_to(st, (GLA_CHUNK, GLA_KW)) for st in starts], axis=0)
    exp_start_full = jnp.concatenate(
        [jnp.broadcast_to(jnp.exp(st), (GLA_CHUNK, GLA_KW)) for st in starts], axis=0)
    q_loc = q * (jnp.exp(cum - start_full) * (GLA_DK ** -0.5))
    q_in = _bf16(q_loc)
    q_tile = _bf16(q_loc * exp_start_full)
    k_end = _bf16(k * jnp.exp(cum_end - cum))
    k_rel = []
    for c in range(n_chunks):
        n = _round_up((c + 1) * GLA_CHUNK, LANE)
        k_rel.append(_bf16(k[:n, :] * jnp.exp(starts[c] - cum[:n, :])))
    tile_decay = jnp.exp(cum_end)
    ext = jnp.concatenate([carry_ref[s], m], axis=0)
    carry_ref[s] = m[ts - SUBLANE:, :]
    m1 = pltpu.roll(ext, 1, 0)[SUBLANE:, :]
    m2 = pltpu.roll(ext, 2, 0)[SUBLANE:, :]
    cw = conv_w_ref[...]
    u = proj_b(OFF_CB, CONV_WIDTH) * (cw[0:1, :] * m2 + cw[1:2, :] * m1 + cw[2:3, :] * m)
    yield

    attn, s_old = [], []
    for hd in range(GLA_HEADS):
        ks = slice(hd * GLA_DK, (hd + 1) * GLA_DK)
        vs = slice(hd * GLA_DV, (hd + 1) * GLA_DV)
        s_t = state_ref[s, hd]
        s_old.append(_bf16(s_t))
        blocks = []
        for c in range(n_chunks):
            rs = slice(c * GLA_CHUNK, (c + 1) * GLA_CHUNK)
            a_c = lax.dot_general(q_in[rs, ks], k_rel[c][:, ks], _NT,
                                  preferred_element_type=F32)
            if a_c.shape[1] < ts:
                a_c = jnp.concatenate(
                    [a_c, jnp.zeros((GLA_CHUNK, ts - a_c.shape[1]), F32)], axis=1)
            blocks.append(a_c)
        attn.append(_bf16(jnp.where(causal, jnp.concatenate(blocks, axis=0), 0.0)))
        state_ref[s, hd] = s_t * tile_decay[:, ks] + lax.dot_general(
            v[:, vs], k_end[:, ks], _TN, preferred_element_type=F32)
    z_c = proj_b(OFF_ZC, CONV_WIDTH)
    gated_c = _bf16(u * (z_c * _sigmoid(z_c)))
    yield
    g_gla = g_gla_ref[...]
    gated = []
    for hd in range(GLA_HEADS):
        ks = slice(hd * GLA_DK, (hd + 1) * GLA_DK)
        vs = slice(hd * GLA_DV, (hd + 1) * GLA_DV)
        o = (lax.dot_general(q_tile[:, ks], s_old[hd], _NT, preferred_element_type=F32)
             + jnp.dot(attn[hd], v[:, vs], preferred_element_type=F32))
        gated.append(_bf16(_rms(o, g_gla) * act_a[:, vs]))
    b_merge = b_merge_ref[...]
    gate_c = _bf16(_sigmoid(proj_b(OFF_GC, D_MODEL) + b_merge[:, D_MODEL:]))
    yield
    y_c = _bf16(gate_c * jnp.dot(gated_c, w_c_ref[...], preferred_element_type=F32))
    gate_a = _bf16(_sigmoid(proj_b(OFF_GA, D_MODEL) + b_merge[:, :D_MODEL]))
    yield
    y_a = jnp.dot(jnp.concatenate(gated, axis=1), w_a_ref[...], preferred_element_type=F32)
    merged = _bf16(gate_a * y_a + y_c)
    yield

    x1 = x + jnp.dot(merged, w_out_ref[...], preferred_element_type=F32)
    yield
    hn = _bf16(_rms(x1, g_ple_ref[...]))
    yield
    ple_gate = _sigmoid(jnp.dot(hn, w_pg_ref[...], preferred_element_type=F32))
    x2 = x1 + ple_gate * jnp.dot(_bf16(p_ref[s]), w_pe_ref[...], preferred_element_type=F32)
    yield
    if apply_final_norm:
        x2 = _rms(x2, g_fin_ref[...])
    out_ref[s] = x2


def _layer_kernel(*refs, apply_final_norm):
    x_ref = refs[0]
    state_ref, carry_ref = refs[-2], refs[-1]

    @pl.when(pl.program_id(1) == 0)
    def _():
        state_ref[...] = jnp.zeros_like(state_ref)
        carry_ref[...] = jnp.zeros_like(carry_ref)

    live = [_tile_stages(s, *refs, apply_final_norm) for s in range(x_ref.shape[0])]
    while live:
        for g in list(live):
            try:
                next(g)
            except StopIteration:
                live.remove(g)


def _const_spec(shape):
    zeros = (0,) * len(shape)
    return pl.BlockSpec(shape, lambda b, t: zeros, pipeline_mode=pl.Buffered(1))


def _layer(x, p_i, weights, vectors, *, apply_final_norm):
    bsz, seq, d = x.shape
    assert d == D_MODEL and seq % SEQ_TILE == 0 and SEQ_TILE % LANE == 0
    nb = 2 if bsz % 2 == 0 else 1
    tile = lambda width: pl.BlockSpec((nb, SEQ_TILE, width), lambda b, t: (b, t, 0))
    in_specs = ([tile(D_MODEL), tile(PLE_DIM)]
                + [_const_spec(w.shape) for w in weights]
                + [_const_spec(vv.shape) for vv in vectors])
    return pl.pallas_call(
        functools.partial(_layer_kernel, apply_final_norm=apply_final_norm),
        grid=(bsz // nb, seq // SEQ_TILE),
        in_specs=in_specs,
        out_specs=tile(D_MODEL),
        out_shape=jax.ShapeDtypeStruct(x.shape, jnp.float32),
        scratch_shapes=[
            pltpu.VMEM((nb, GLA_HEADS, GLA_DV, GLA_DK), jnp.float32),
            pltpu.VMEM((nb, SUBLANE, CONV_WIDTH), jnp.float32),
        ],
        compiler_params=pltpu.CompilerParams(
            dimension_semantics=("arbitrary", "arbitrary"),
            vmem_limit_bytes=VMEM_LIMIT_BYTES),
        name="hybrid_layer",
    )(x, p_i, *weights, *vectors)


def kernel(x, p, norm_mix_g, w_in, b_merge, w_gk2, b_gk, gla_norm_g, conv_w, w_branch_a,
           w_branch_c, w_out, norm_ple_g, w_ple_gate, w_ple_proj, norm_final_g):
    depth = w_in.shape[0]
    row = lambda a: a.reshape(1, -1).astype(jnp.float32)
    for i in range(depth):
        weights = _prep_weights(i, w_in, w_gk2, w_branch_a, w_branch_c, w_out, w_ple_gate,
                                w_ple_proj)
        vectors = [row(norm_mix_g[i]), row(b_gk[i]), row(gla_norm_g[i]),
                   conv_w[i].astype(jnp.float32), row(b_merge[i]), row(norm_ple_g[i]),
                   row(norm_final_g)]
        x = _layer(x, p[i], weights, vectors, apply_final_norm=(i == depth - 1))
    return x
```

```python
import functools

import jax
import jax.numpy as jnp
from jax import lax
from jax.experimental import pallas as pl
from jax.experimental.pallas import tpu as pltpu

D_MODEL = 1024
GLA_HEADS = 4
GLA_DK = 128
GLA_DV = 256
GLA_KW = GLA_HEADS * GLA_DK
GLA_VW = GLA_HEADS * GLA_DV
GATE_RANK = 16
GATE_NORMALIZER = 16.0
GLA_CHUNK = 64
CONV_WIDTH = 1024
CONV_K = 3
PLE_DIM = 256
EPS = 1e-6

LANE = 128
SUBLANE = 8
SEQ_TILE = 256
COL_CHUNK = 256
PREP_STEPS = 8
VMEM_LIMIT_BYTES = 58 * 1024 * 1024

OFF_Q = 0
OFF_K = OFF_Q + GLA_KW
OFF_V = OFF_K + GLA_KW
OFF_ZA = OFF_V + GLA_VW
COLS_A = OFF_ZA + GLA_VW
OFF_CB = 0
OFF_CC = OFF_CB + CONV_WIDTH
OFF_XC = OFF_CC + CONV_WIDTH
OFF_ZC = OFF_XC + CONV_WIDTH
OFF_GA = OFF_ZC + CONV_WIDTH
OFF_GC = OFF_GA + D_MODEL
COLS_B = OFF_GC + D_MODEL
IN_COLS = COLS_A + GATE_RANK + COLS_B

_NT = (((1,), (1,)), ((), ()))
_TN = (((0,), (0,)), ((), ()))
F32 = jnp.float32


def _bf16(a):
    return a.astype(jnp.bfloat16)


def _rms(xf, g):
    ms = jnp.mean(xf * xf, axis=-1, keepdims=True)
    return xf * lax.rsqrt(ms + EPS) * g


def _sigmoid(a):
    return 0.5 * jnp.tanh(0.5 * a) + 0.5


def _log_sigmoid(a):
    return jnp.minimum(a, 0.0) - jnp.log(1.0 + jnp.exp(-jnp.abs(a)))


def _round_up(n, m):
    return (n + m - 1) // m * m


def _prep_kernel(wt_a_ref, wt_gate_ref, wt_b_ref, wt_b_next_ref, w_gk2_ref, wa_ref, wc_ref,
                 wo_ref, wpg_ref, wpe_ref,
                 a_out, gkl_out, b_out, gk2_out, wa_out, wc_out, wo_out, wpg_out, wpe_out):
    a_out[...] = _bf16(wt_a_ref[...].T)
    rows_b = jnp.concatenate([wt_b_ref[GATE_RANK:, :], wt_b_next_ref[...]], axis=0)
    b_out[...] = _bf16(rows_b.T)
    wa_out[...] = _bf16(wa_ref[...])
    wc_out[...] = _bf16(wc_ref[...])
    wo_out[...] = _bf16(wo_ref[...])
    wpg_out[...] = _bf16(wpg_ref[...])
    wpe_out[...] = _bf16(wpe_ref[...])

    @pl.when(pl.program_id(0) == 0)
    def _():
        gate_rows = jnp.concatenate(
            [wt_gate_ref[...], jnp.zeros((LANE - GATE_RANK, D_MODEL), F32)], axis=0)
        gkl_out[...] = _bf16(gate_rows.T)
        gk2_out[...] = jnp.zeros_like(gk2_out)
        gk2_out[:GATE_RANK, :] = _bf16(w_gk2_ref[...])


def _prep_weights(layer, w_in, w_gk2, w_branch_a, w_branch_c, w_out, w_ple_gate, w_ple_proj):
    assert w_in.shape[1:] == (D_MODEL, IN_COLS)
    w_in_t = jnp.swapaxes(w_in, 1, 2)
    rows_a, rows_b = COLS_A // PREP_STEPS, COLS_B // PREP_STEPS
    assert COLS_A % rows_b == 0 and COLS_A % GATE_RANK == 0 and rows_b % GATE_RANK == 0

    def rows(a):
        r, c = a.shape[1] // PREP_STEPS, a.shape[2]
        return pl.BlockSpec((None, r, c), lambda i: (layer, i, 0))

    def out_rows(r, c):
        return pl.BlockSpec((r // PREP_STEPS, c), lambda i: (i, 0))

    def out_cols(r, c):
        return pl.BlockSpec((r, c // PREP_STEPS), lambda i: (0, i))

    in_specs = [
        pl.BlockSpec((None, rows_a, D_MODEL), lambda i: (layer, i, 0)),
        pl.BlockSpec((None, GATE_RANK, D_MODEL), lambda i: (layer, COLS_A // GATE_RANK, 0)),
        pl.BlockSpec((None, rows_b, D_MODEL), lambda i: (layer, COLS_A // rows_b + i, 0)),
        pl.BlockSpec((None, GATE_RANK, D_MODEL),
                     lambda i: (layer, (COLS_A + rows_b * (i + 1)) // GATE_RANK, 0)),
        pl.BlockSpec((None, GATE_RANK, GLA_KW), lambda i: (layer, 0, 0)),
        rows(w_branch_a), rows(w_branch_c), rows(w_out), rows(w_ple_gate), rows(w_ple_proj)]
    shapes = [(D_MODEL, COLS_A), (D_MODEL, LANE), (D_MODEL, COLS_B), (LANE, GLA_KW),
              (GLA_VW, D_MODEL), (CONV_WIDTH, D_MODEL), (D_MODEL, D_MODEL), (D_MODEL, D_MODEL),
              (PLE_DIM, D_MODEL)]
    out_specs = [out_cols(*shapes[0]),
                 pl.BlockSpec(shapes[1], lambda i: (0, 0)),
                 out_cols(*shapes[2]),
                 pl.BlockSpec(shapes[3], lambda i: (0, 0))]
    out_specs += [out_rows(*sh) for sh in shapes[4:]]
    return pl.pallas_call(
        _prep_kernel,
        grid=(PREP_STEPS,),
        in_specs=in_specs,
        out_specs=out_specs,
        out_shape=[jax.ShapeDtypeStruct(sh, jnp.bfloat16) for sh in shapes],
        compiler_params=pltpu.CompilerParams(dimension_semantics=("arbitrary",)),
        name="prep_weights",
    )(w_in_t, w_in_t, w_in_t, w_in_t, w_gk2, w_branch_a, w_branch_c, w_out, w_ple_gate,
      w_ple_proj)


def _tile_stages(s, x_ref, p_ref, w_a_in_ref, w_gkl_ref, w_b_in_ref, w_gk2_ref, w_a_ref,
                 w_c_ref, w_out_ref, w_pg_ref, w_pe_ref, g_mix_ref, b_gk_ref, g_gla_ref,
                 conv_w_ref, b_merge_ref, g_ple_ref, g_fin_ref, out_ref, state_ref, carry_ref,
                 apply_final_norm):
    ts = x_ref.shape[1]
    n_chunks = ts // GLA_CHUNK

    x = x_ref[s]
    h = _bf16(_rms(x, g_mix_ref[...]))
    yield

    def proj_a(off, width):
        return jnp.dot(h, w_a_in_ref[:, off:off + width], preferred_element_type=F32)

    def proj_b(off, width):
        return jnp.dot(h, w_b_in_ref[:, off:off + width], preferred_element_type=F32)

    gk_low = _bf16(jnp.dot(h, w_gkl_ref[...], preferred_element_type=F32))
    q = proj_a(OFF_Q, GLA_KW)
    k = proj_a(OFF_K, GLA_KW)
    yield
    pre = jnp.dot(gk_low, w_gk2_ref[...], preferred_element_type=F32) + b_gk_ref[...]
    v = _bf16(proj_a(OFF_V, GLA_VW))
    yield
    gk = _log_sigmoid(pre) * (1.0 / GATE_NORMALIZER)
    gk_hi = _bf16(gk)
    gk_lo = _bf16(gk - gk_hi.astype(F32))
    z_a = proj_a(OFF_ZA, GLA_VW)
    act_a = _bf16(z_a * _sigmoid(z_a))
    yield
    row = lax.broadcasted_iota(jnp.int32, (ts, ts), 0)
    col = lax.broadcasted_iota(jnp.int32, (ts, ts), 1)
    causal = col <= row
    tri = jnp.where(causal, 1.0, 0.0).astype(jnp.bfloat16)
    cum = (jnp.dot(tri, gk_hi, preferred_element_type=F32)
           + jnp.dot(tri, gk_lo, preferred_element_type=F32))
    yield
    cum_end = cum[ts - 1:ts, :]
    starts = [jnp.zeros((1, GLA_KW), F32)]
    starts += [cum[c * GLA_CHUNK - 1:c * GLA_CHUNK, :] for c in range(1, n_chunks)]
    start_full = jnp.concatenate(
        [jnp.broadcast_to(st, (GLA_CHUNK, GLA_KW)) for st in starts], axis=0)
    exp_start_full = jnp.concatenate(
        [jnp.broadcast_to(jnp.exp(st), (GLA_CHUNK, GLA_KW)) for st in starts], axis=0)
    q_loc = q * (jnp.exp(cum - start_full) * (GLA_DK ** -0.5))
    q_in = _bf16(q_loc)
    q_tile = _bf16(q_loc * exp_start_full)
    k_end = _bf16(k * jnp.exp(cum_end - cum))
    k_rel = []
    for c in range(n_chunks):
        n = _round_up((c + 1) * GLA_CHUNK, LANE)
        k_rel.append(_bf16(k[:n, :] * jnp.exp(starts[c] - cum[:n, :])))
    tile_decay = jnp.exp(cum_end)
    yield

    attn, s_old = [], []
    for hd in range(GLA_HEADS):
        ks = slice(hd * GLA_DK, (hd + 1) * GLA_DK)
        vs = slice(hd * GLA_DV, (hd + 1) * GLA_DV)
        s_t = state_ref[s, hd]
        s_old.append(_bf16(s_t))
        blocks = []
        for c in range(n_chunks):
            rs = slice(c * GLA_CHUNK, (c + 1) * GLA_CHUNK)
            a_c = lax.dot_general(q_in[rs, ks], k_rel[c][:, ks], _NT,
                                  preferred_element_type=F32)
            if a_c.shape[1] < ts:
                a_c = jnp.concatenate(
                    [a_c, jnp.zeros((GLA_CHUNK, ts - a_c.shape[1]), F32)], axis=1)
            blocks.append(a_c)
        attn.append(_bf16(jnp.where(causal, jnp.concatenate(blocks, axis=0), 0.0)))
        state_ref[s, hd] = s_t * tile_decay[:, ks] + lax.dot_general(
            v[:, vs], k_end[:, ks], _TN, preferred_element_type=F32)
    gated_c = []
    for c0 in range(0, CONV_WIDTH, COL_CHUNK):
        cs = slice(c0, c0 + COL_CHUNK)
        m = proj_b(OFF_CC + c0, COL_CHUNK) * proj_b(OFF_XC + c0, COL_CHUNK)
        ext = jnp.concatenate([carry_ref[s, :, cs], m], axis=0)
        carry_ref[s, :, cs] = m[ts - SUBLANE:, :]
        m1 = pltpu.roll(ext, 1, 0)[SUBLANE:, :]
        m2 = pltpu.roll(ext, 2, 0)[SUBLANE:, :]
        cw = conv_w_ref[:, cs]
        u = proj_b(OFF_CB + c0, COL_CHUNK) * (cw[0:1, :] * m2 + cw[1:2, :] * m1 + cw[2:3, :] * m)
        z_c = proj_b(OFF_ZC + c0, COL_CHUNK)
        gated_c.append(_bf16(u * (z_c * _sigmoid(z_c))))
    gated_c = jnp.concatenate(gated_c, axis=1)
    yield
    g_gla = g_gla_ref[...]
    gated = []
    for hd in range(GLA_HEADS):
        ks = slice(hd * GLA_DK, (hd + 1) * GLA_DK)
        vs = slice(hd * GLA_DV, (hd + 1) * GLA_DV)
        o = (lax.dot_general(q_tile[:, ks], s_old[hd], _NT, preferred_element_type=F32)
             + jnp.dot(attn[hd], v[:, vs], preferred_element_type=F32))
        gated.append(_bf16(_rms(o, g_gla) * act_a[:, vs]))
    gated = jnp.concatenate(gated, axis=1)
    yield
    merged = []
    for c0 in range(0, D_MODEL, COL_CHUNK):
        cs = slice(c0, c0 + COL_CHUNK)
        gate_c = _sigmoid(proj_b(OFF_GC + c0, COL_CHUNK)
                          + b_merge_ref[:, D_MODEL + c0:D_MODEL + c0 + COL_CHUNK])
        y_c = gate_c * jnp.dot(gated_c, w_c_ref[:, cs], preferred_element_type=F32)
        gate_a = _sigmoid(proj_b(OFF_GA + c0, COL_CHUNK) + b_merge_ref[:, cs])
        y_a = gate_a * jnp.dot(gated, w_a_ref[:, cs], preferred_element_type=F32)
        merged.append(_bf16(y_a + y_c))
        if c0 == D_MODEL // 2 - COL_CHUNK:
            yield
    merged = jnp.concatenate(merged, axis=1)
    yield

    x1 = x + jnp.dot(merged, w_out_ref[...], preferred_element_type=F32)
    yield
    hn = _bf16(_rms(x1, g_ple_ref[...]))
    yield
    ple_gate = _sigmoid(jnp.dot(hn, w_pg_ref[...], preferred_element_type=F32))
    x2 = x1 + ple_gate * jnp.dot(_bf16(p_ref[s]), w_pe_ref[...], preferred_element_type=F32)
    yield
    if apply_final_norm:
        x2 = _rms(x2, g_fin_ref[...])
    out_ref[s] = x2


def _layer_kernel(*refs, apply_final_norm):
    x_ref = refs[0]
    state_ref, carry_ref = refs[-2], refs[-1]

    @pl.when(pl.program_id(1) == 0)
    def _():
        state_ref[...] = jnp.zeros_like(state_ref)
        carry_ref[...] = jnp.zeros_like(carry_ref)

    live = [_tile_stages(s, *refs, apply_final_norm) for s in range(x_ref.shape[0])]
    while live:
        for g in list(live):
            try:
                next(g)
            except StopIteration:
                live.remove(g)


def _const_spec(shape):
    zeros = (0,) * len(shape)
    return pl.BlockSpec(shape, lambda b, t: zeros, pipeline_mode=pl.Buffered(1))


def _layer(x, p_i, weights, vectors, *, apply_final_norm):
    bsz, seq, d = x.shape
    assert d == D_MODEL and seq % SEQ_TILE == 0 and SEQ_TILE % LANE == 0
    nb = 2 if bsz % 2 == 0 else 1
    tile = lambda width: pl.BlockSpec((nb, SEQ_TILE, width), lambda b, t: (b, t, 0))
    in_specs = ([tile(D_MODEL), tile(PLE_DIM)]
                + [_const_spec(w.shape) for w in weights]
                + [_const_spec(vv.shape) for vv in vectors])
    return pl.pallas_call(
        functools.partial(_layer_kernel, apply_final_norm=apply_final_norm),
        grid=(bsz // nb, seq // SEQ_TILE),
        in_specs=in_specs,
        out_specs=tile(D_MODEL),
        out_shape=jax.ShapeDtypeStruct(x.shape, jnp.float32),
        scratch_shapes=[
            pltpu.VMEM((nb, GLA_HEADS, GLA_DV, GLA_DK), jnp.float32),
            pltpu.VMEM((nb, SUBLANE, CONV_WIDTH), jnp.float32),
        ],
        compiler_params=pltpu.CompilerParams(
            dimension_semantics=("arbitrary", "arbitrary"),
            vmem_limit_bytes=VMEM_LIMIT_BYTES),
        name="hybrid_layer",
    )(x, p_i, *weights, *vectors)


def kernel(x, p, norm_mix_g, w_in, b_merge, w_gk2, b_gk, gla_norm_g, conv_w, w_branch_a,
           w_branch_c, w_out, norm_ple_g, w_ple_gate, w_ple_proj, norm_final_g):
    depth = w_in.shape[0]
    row = lambda a: a.reshape(1, -1).astype(jnp.float32)
    for i in range(depth):
        weights = _prep_weights(i, w_in, w_gk2, w_branch_a, w_branch_c, w_out, w_ple_gate,
                                w_ple_proj)
        vectors = [row(norm_mix_g[i]), row(b_gk[i]), row(gla_norm_g[i]),
                   conv_w[i].astype(jnp.float32), row(b_merge[i]), row(norm_ple_g[i]),
                   row(norm_final_g)]
        x = _layer(x, p[i], weights, vectors, apply_final_norm=(i == depth - 1))
    return x
```

```python
import functools

import jax
import jax.numpy as jnp
from jax import lax
from jax.experimental import pallas as pl
from jax.experimental.pallas import tpu as pltpu

D_MODEL = 1024
GLA_HEADS = 4
GLA_DK = 128
GLA_DV = 256
GLA_KW = GLA_HEADS * GLA_DK
GLA_VW = GLA_HEADS * GLA_DV
GATE_RANK = 16
GATE_NORMALIZER = 16.0
GLA_CHUNK = 64
CONV_WIDTH = 1024
CONV_K = 3
PLE_DIM = 256
EPS = 1e-6

LANE = 128
SUBLANE = 8
SEQ_TILE = 256
WCHUNK = 256
VMEM_LIMIT_BYTES = 58 * 1024 * 1024

OFF_Q = 0
OFF_K = OFF_Q + GLA_KW
OFF_V = OFF_K + GLA_KW
OFF_ZA = OFF_V + GLA_VW
COLS_A = OFF_ZA + GLA_VW
OFF_CB = 0
OFF_CC = OFF_CB + CONV_WIDTH
OFF_XC = OFF_CC + CONV_WIDTH
OFF_ZC = OFF_XC + CONV_WIDTH
OFF_GA = OFF_ZC + CONV_WIDTH
OFF_GC = OFF_GA + D_MODEL
COLS_B = OFF_GC + D_MODEL
IN_COLS = COLS_A + GATE_RANK + COLS_B

_NT = (((1,), (1,)), ((), ()))
_TN = (((0,), (0,)), ((), ()))
F32 = jnp.float32


def _bf16(a):
    return a.astype(jnp.bfloat16)


def _rms(xf, g):
    ms = jnp.mean(xf * xf, axis=-1, keepdims=True)
    return xf * lax.rsqrt(ms + EPS) * g


def _sigmoid(a):
    return 0.5 * jnp.tanh(0.5 * a) + 0.5


def _log_sigmoid(a):
    return jnp.minimum(a, 0.0) - jnp.log(1.0 + jnp.exp(-jnp.abs(a)))


def _round_up(n, m):
    return (n + m - 1) // m * m


def _stream_chunks(n_chunks, src_chunk, consume, stage_ref, sem_ref):
    def copy(c, slot):
        return pltpu.make_async_copy(src_chunk(c), stage_ref.at[slot], sem_ref.at[slot])

    copy(0, 0).start()

    def body(c, carry):
        slot = c % 2
        copy(c, slot).wait()

        @pl.when(c + 1 < n_chunks)
        def _():
            copy(c + 1, 1 - slot).start()

        consume(c, stage_ref[slot])
        return carry

    lax.fori_loop(0, n_chunks, body, 0)


def _load_weights(layer, w_in_t_hbm, wa_hbm, wc_hbm, wo_hbm, wpg_hbm, wpe_hbm, gate_rows_ref,
                  w_gk2_f32_ref, w_a_in_ref, w_gkl_ref, w_b_in_ref, w_gk2_ref, w_a_ref, w_c_ref,
                  w_out_ref, w_pg_ref, w_pe_ref, stage_ref, sem_ref):
    def in_proj_rows(first_row):
        return lambda c: w_in_t_hbm.at[layer, pl.ds(first_row + c * WCHUNK, WCHUNK), :]

    def transposed_into(dst_ref):
        def consume(c, chunk):
            dst_ref[c] = _bf16(chunk.T)
        return consume

    def rows_into(dst_ref):
        def consume(c, chunk):
            dst_ref[pl.ds(pl.multiple_of(c * WCHUNK, WCHUNK), WCHUNK), :] = _bf16(chunk)
        return consume

    _stream_chunks(COLS_A // WCHUNK, in_proj_rows(0), transposed_into(w_a_in_ref),
                   stage_ref, sem_ref)
    _stream_chunks(COLS_B // WCHUNK, in_proj_rows(COLS_A + GATE_RANK),
                   transposed_into(w_b_in_ref), stage_ref, sem_ref)
    for hbm, dst in ((wa_hbm, w_a_ref), (wc_hbm, w_c_ref), (wo_hbm, w_out_ref),
                     (wpg_hbm, w_pg_ref), (wpe_hbm, w_pe_ref)):
        _stream_chunks(dst.shape[0] // WCHUNK,
                       lambda c, hbm=hbm: hbm.at[layer, pl.ds(c * WCHUNK, WCHUNK), :],
                       rows_into(dst), stage_ref, sem_ref)
    gate_rows = jnp.concatenate(
        [gate_rows_ref[...], jnp.zeros((LANE - GATE_RANK, D_MODEL), F32)], axis=0)
    w_gkl_ref[...] = _bf16(gate_rows.T)
    w_gk2_ref[...] = jnp.zeros_like(w_gk2_ref)
    w_gk2_ref[:GATE_RANK, :] = _bf16(w_gk2_f32_ref[...])


def _tile_stages(s, x_ref, p_ref, w_a_in_ref, w_gkl_ref, w_b_in_ref, w_gk2_ref, w_a_ref,
                 w_c_ref, w_out_ref, w_pg_ref, w_pe_ref, g_mix_ref, b_gk_ref, g_gla_ref,
                 conv_w_ref, b_merge_ref, g_ple_ref, g_fin_ref, out_ref, state_ref, carry_ref,
                 apply_final_norm):
    ts = x_ref.shape[1]
    n_chunks = ts // GLA_CHUNK

    x = x_ref[s]
    h = _bf16(_rms(x, g_mix_ref[...]))
    yield

    def proj(w_ref, off, width):
        return jnp.concatenate(
            [jnp.dot(h, w_ref[c], preferred_element_type=F32)
             for c in range(off // WCHUNK, (off + width) // WCHUNK)], axis=1)

    proj_a = functools.partial(proj, w_a_in_ref)
    proj_b = functools.partial(proj, w_b_in_ref)

    gk_low = _bf16(jnp.dot(h, w_gkl_ref[...], preferred_element_type=F32))
    q = proj_a(OFF_Q, GLA_KW)
    k = proj_a(OFF_K, GLA_KW)
    yield
    pre = jnp.dot(gk_low, w_gk2_ref[...], preferred_element_type=F32) + b_gk_ref[...]
    v = _bf16(proj_a(OFF_V, GLA_VW))
    yield
    gk = _log_sigmoid(pre) * (1.0 / GATE_NORMALIZER)
    gk_hi = _bf16(gk)
    gk_lo = _bf16(gk - gk_hi.astype(F32))
    z_a = proj_a(OFF_ZA, GLA_VW)
    act_a = _bf16(z_a * _sigmoid(z_a))
    yield
    row = lax.broadcasted_iota(jnp.int32, (ts, ts), 0)
    col = lax.broadcasted_iota(jnp.int32, (ts, ts), 1)
    causal = col <= row
    tri = jnp.where(causal, 1.0, 0.0).astype(jnp.bfloat16)
    cum = (jnp.dot(tri, gk_hi, preferred_element_type=F32)
           + jnp.dot(tri, gk_lo, preferred_element_type=F32))
    m = proj_b(OFF_CC, CONV_WIDTH) * proj_b(OFF_XC, CONV_WIDTH)
    yield
    cum_end = cum[ts - 1:ts, :]
    starts = [jnp.zeros((1, GLA_KW), F32)]
    starts += [cum[c * GLA_CHUNK - 1:c * GLA_CHUNK, :] for c in range(1, n_chunks)]
    start_full = jnp.concatenate(
        [jnp.broadcast_to(st, (GLA_CHUNK, GLA_KW)) for st in starts], axis=0)
    exp_start_full = jnp.concatenate(
        [jnp.broadcast_to(jnp.exp(st), (GLA_CHUNK, GLA_KW)) for st in starts], axis=0)
    q_loc = q * (jnp.exp(cum - start_full) * (GLA_DK ** -0.5))
    q_in = _bf16(q_loc)
    q_tile = _bf16(q_loc * exp_start_full)
    k_end = _bf16(k * jnp.exp(cum_end - cum))
    k_rel = []
    for c in range(n_chunks):
        n = _round_up((c + 1) * GLA_CHUNK, LANE)
        k_rel.append(_bf16(k[:n, :] * jnp.exp(starts[c] - cum[:n, :])))
    tile_decay = jnp.exp(cum_end)
    ext = jnp.concatenate([carry_ref[s], m], axis=0)
    carry_ref[s] = m[ts - SUBLANE:, :]
    m1 = pltpu.roll(ext, 1, 0)[SUBLANE:, :]
    m2 = pltpu.roll(ext, 2, 0)[SUBLANE:, :]
    cw = conv_w_ref[...]
    u = proj_b(OFF_CB, CONV_WIDTH) * (cw[0:1, :] * m2 + cw[1:2, :] * m1 + cw[2:3, :] * m)
    yield

    attn, s_old = [], []
    for hd in range(GLA_HEADS):
        ks = slice(hd * GLA_DK, (hd + 1) * GLA_DK)
        vs = slice(hd * GLA_DV, (hd + 1) * GLA_DV)
        s_t = state_ref[s, hd]
        s_old.append(_bf16(s_t))
        blocks = []
        for c in range(n_chunks):
            rs = slice(c * GLA_CHUNK, (c + 1) * GLA_CHUNK)
            a_c = lax.dot_general(q_in[rs, ks], k_rel[c][:, ks], _NT,
                                  preferred_element_type=F32)
            if a_c.shape[1] < ts:
                a_c = jnp.concatenate(
                    [a_c, jnp.zeros((GLA_CHUNK, ts - a_c.shape[1]), F32)], axis=1)
            blocks.append(a_c)
        attn.append(_bf16(jnp.where(causal, jnp.concatenate(blocks, axis=0), 0.0)))
        state_ref[s, hd] = s_t * tile_decay[:, ks] + lax.dot_general(
            v[:, vs], k_end[:, ks], _TN, preferred_element_type=F32)
    z_c = proj_b(OFF_ZC, CONV_WIDTH)
    gated_c = _bf16(u * (z_c * _sigmoid(z_c)))
    yield
    g_gla = g_gla_ref[...]
    gated = []
    for hd in range(GLA_HEADS):
        ks = slice(hd * GLA_DK, (hd + 1) * GLA_DK)
        vs = slice(hd * GLA_DV, (hd + 1) * GLA_DV)
        o = (lax.dot_general(q_tile[:, ks], s_old[hd], _NT, preferred_element_type=F32)
             + jnp.dot(attn[hd], v[:, vs], preferred_element_type=F32))
        gated.append(_bf16(_rms(o, g_gla) * act_a[:, vs]))
    b_merge = b_merge_ref[...]
    gate_c = _bf16(_sigmoid(proj_b(OFF_GC, D_MODEL) + b_merge[:, D_MODEL:]))
    yield
    y_c = _bf16(gate_c * jnp.dot(gated_c, w_c_ref[...], preferred_element_type=F32))
    gate_a = _bf16(_sigmoid(proj_b(OFF_GA, D_MODEL) + b_merge[:, :D_MODEL]))
    yield
    y_a = jnp.dot(jnp.concatenate(gated, axis=1), w_a_ref[...], preferred_element_type=F32)
    merged = _bf16(gate_a * y_a + y_c)
    yield

    x1 = x + jnp.dot(merged, w_out_ref[...], preferred_element_type=F32)
    yield
    hn = _bf16(_rms(x1, g_ple_ref[...]))
    yield
    ple_gate = _sigmoid(jnp.dot(hn, w_pg_ref[...], preferred_element_type=F32))
    x2 = x1 + ple_gate * jnp.dot(_bf16(p_ref[s]), w_pe_ref[...], preferred_element_type=F32)
    yield
    if apply_final_norm:
        x2 = _rms(x2, g_fin_ref[...])
    out_ref[s] = x2


def _layer_kernel(x_ref, p_ref, w_in_t_hbm, wa_hbm, wc_hbm, wo_hbm, wpg_hbm, wpe_hbm,
                  gate_rows_ref, w_gk2_f32_ref, g_mix_ref, b_gk_ref, g_gla_ref, conv_w_ref,
                  b_merge_ref, g_ple_ref, g_fin_ref, out_ref,
                  w_a_in_ref, w_gkl_ref, w_b_in_ref, w_gk2_ref, w_a_ref, w_c_ref, w_out_ref,
                  w_pg_ref, w_pe_ref, stage_ref, sem_ref, state_ref, carry_ref,
                  *, layer, apply_final_norm):
    @pl.when((pl.program_id(0) == 0) & (pl.program_id(1) == 0))
    def _():
        _load_weights(layer, w_in_t_hbm, wa_hbm, wc_hbm, wo_hbm, wpg_hbm, wpe_hbm, gate_rows_ref,
                      w_gk2_f32_ref, w_a_in_ref, w_gkl_ref, w_b_in_ref, w_gk2_ref, w_a_ref,
                      w_c_ref, w_out_ref, w_pg_ref, w_pe_ref, stage_ref, sem_ref)

    @pl.when(pl.program_id(1) == 0)
    def _():
        state_ref[...] = jnp.zeros_like(state_ref)
        carry_ref[...] = jnp.zeros_like(carry_ref)

    live = [_tile_stages(s, x_ref, p_ref, w_a_in_ref, w_gkl_ref, w_b_in_ref, w_gk2_ref, w_a_ref,
                         w_c_ref, w_out_ref, w_pg_ref, w_pe_ref, g_mix_ref, b_gk_ref, g_gla_ref,
                         conv_w_ref, b_merge_ref, g_ple_ref, g_fin_ref, out_ref, state_ref,
                         carry_ref, apply_final_norm)
            for s in range(x_ref.shape[0])]
    while live:
        for g in list(live):
            try:
                next(g)
            except StopIteration:
                live.remove(g)


def _const_spec(shape, index=None):
    index = (0,) * len(shape) if index is None else index
    return pl.BlockSpec(shape, lambda b, t: index, pipeline_mode=pl.Buffered(1))


def _layer(layer, x, p_i, w_in, w_gk2, big_weights, vectors, *, apply_final_norm):
    bsz, seq, d = x.shape
    assert d == D_MODEL and seq % SEQ_TILE == 0 and SEQ_TILE % LANE == 0
    assert w_in.shape[1:] == (D_MODEL, IN_COLS) and COLS_A % GATE_RANK == 0
    nb = 2 if bsz % 2 == 0 else 1
    w_in_t = jnp.swapaxes(w_in, 1, 2)
    tile = lambda width: pl.BlockSpec((nb, SEQ_TILE, width), lambda b, t: (b, t, 0))
    in_hbm = pl.BlockSpec(memory_space=pltpu.HBM)
    in_specs = ([tile(D_MODEL), tile(PLE_DIM)] + [in_hbm] * (1 + len(big_weights))
                + [_const_spec((None, GATE_RANK, D_MODEL), (layer, COLS_A // GATE_RANK, 0)),
                   _const_spec((None, GATE_RANK, GLA_KW), (layer, 0, 0))]
                + [_const_spec(vv.shape) for vv in vectors])
    bf16 = jnp.bfloat16
    return pl.pallas_call(
        functools.partial(_layer_kernel, layer=layer, apply_final_norm=apply_final_norm),
        grid=(bsz // nb, seq // SEQ_TILE),
        in_specs=in_specs,
        out_specs=tile(D_MODEL),
        out_shape=jax.ShapeDtypeStruct(x.shape, jnp.float32),
        scratch_shapes=[
            pltpu.VMEM((COLS_A // WCHUNK, D_MODEL, WCHUNK), bf16),
            pltpu.VMEM((D_MODEL, LANE), bf16),
            pltpu.VMEM((COLS_B // WCHUNK, D_MODEL, WCHUNK), bf16),
            pltpu.VMEM((LANE, GLA_KW), bf16),
            pltpu.VMEM((GLA_VW, D_MODEL), bf16),
            pltpu.VMEM((CONV_WIDTH, D_MODEL), bf16),
            pltpu.VMEM((D_MODEL, D_MODEL), bf16),
            pltpu.VMEM((D_MODEL, D_MODEL), bf16),
            pltpu.VMEM((PLE_DIM, D_MODEL), bf16),
            pltpu.VMEM((2, WCHUNK, D_MODEL), jnp.float32),
            pltpu.SemaphoreType.DMA((2,)),
            pltpu.VMEM((nb, GLA_HEADS, GLA_DV, GLA_DK), jnp.float32),
            pltpu.VMEM((nb, SUBLANE, CONV_WIDTH), jnp.float32),
        ],
        compiler_params=pltpu.CompilerParams(
            dimension_semantics=("arbitrary", "arbitrary"),
            vmem_limit_bytes=VMEM_LIMIT_BYTES),
        name="hybrid_layer",
    )(x, p_i, w_in_t, *big_weights, w_in_t, w_gk2, *vectors)


def kernel(x, p, norm_mix_g, w_in, b_merge, w_gk2, b_gk, gla_norm_g, conv_w, w_branch_a,
           w_branch_c, w_out, norm_ple_g, w_ple_gate, w_ple_proj, norm_final_g):
    depth = w_in.shape[0]
    row = lambda a: a.reshape(1, -1).astype(jnp.float32)
    for i in range(depth):
        vectors = [row(norm_mix_g[i]), row(b_gk[i]), row(gla_norm_g[i]),
                   conv_w[i].astype(jnp.float32), row(b_merge[i]), row(norm_ple_g[i]),
                   row(norm_final_g)]
        x = _layer(i, x, p[i], w_in, w_gk2,
                   [w_branch_a, w_branch_c, w_out, w_ple_gate, w_ple_proj], vectors,
                   apply_final_norm=(i == depth - 1))
    return x
```

```python
import functools

import jax
import jax.numpy as jnp
from jax import lax
from jax.experimental import pallas as pl
from jax.experimental.pallas import tpu as pltpu

D_MODEL = 1024
GLA_HEADS = 4
GLA_DK = 128
GLA_DV = 256
GLA_KW = GLA_HEADS * GLA_DK
GLA_VW = GLA_HEADS * GLA_DV
GATE_RANK = 16
GATE_NORMALIZER = 16.0
GLA_CHUNK = 64
CONV_WIDTH = 1024
CONV_K = 3
PLE_DIM = 256
EPS = 1e-6

LANE = 128
SUBLANE = 8
SEQ_TILE = 256
WCHUNK = 256
WSLOTS = 4
VMEM_LIMIT_BYTES = 58 * 1024 * 1024

OFF_Q = 0
OFF_K = OFF_Q + GLA_KW
OFF_V = OFF_K + GLA_KW
OFF_ZA = OFF_V + GLA_VW
COLS_A = OFF_ZA + GLA_VW
OFF_CB = 0
OFF_CC = OFF_CB + CONV_WIDTH
OFF_XC = OFF_CC + CONV_WIDTH
OFF_ZC = OFF_XC + CONV_WIDTH
OFF_GA = OFF_ZC + CONV_WIDTH
OFF_GC = OFF_GA + D_MODEL
COLS_B = OFF_GC + D_MODEL
IN_COLS = COLS_A + GATE_RANK + COLS_B

_NT = (((1,), (1,)), ((), ()))
_TN = (((0,), (0,)), ((), ()))
F32 = jnp.float32


def _bf16(a):
    return a.astype(jnp.bfloat16)


def _rms(xf, g):
    ms = jnp.mean(xf * xf, axis=-1, keepdims=True)
    return xf * lax.rsqrt(ms + EPS) * g


def _sigmoid(a):
    return 0.5 * jnp.tanh(0.5 * a) + 0.5


def _log_sigmoid(a):
    return jnp.minimum(a, 0.0) - jnp.log(1.0 + jnp.exp(-jnp.abs(a)))


def _round_up(n, m):
    return (n + m - 1) // m * m


def _stream_chunks(n_chunks, src_chunk, consume, stage_ref, sem_ref):
    ahead = WSLOTS - 1

    def copy(c):
        slot = c % WSLOTS
        return pltpu.make_async_copy(src_chunk(c), stage_ref.at[slot], sem_ref.at[slot])

    for c in range(min(ahead, n_chunks)):
        copy(c).start()

    def body(c, carry):
        copy(c).wait()

        @pl.when(c + ahead < n_chunks)
        def _():
            copy(c + ahead).start()

        consume(c, stage_ref[c % WSLOTS])
        return carry

    lax.fori_loop(0, n_chunks, body, 0)


def _load_weights(layer, w_in_t_hbm, wa_hbm, wc_hbm, wo_hbm, wpg_hbm, wpe_hbm, gate_rows_ref,
                  w_gk2_f32_ref, w_a_in_ref, w_gkl_ref, w_b_in_ref, w_gk2_ref, w_a_ref, w_c_ref,
                  w_out_ref, w_pg_ref, w_pe_ref, stage_ref, sem_ref):
    def in_proj_rows(first_row):
        return lambda c: w_in_t_hbm.at[layer, pl.ds(first_row + c * WCHUNK, WCHUNK), :]

    def transposed_into(dst_ref):
        def consume(c, chunk):
            dst_ref[c] = _bf16(chunk.T)
        return consume

    def rows_into(dst_ref):
        def consume(c, chunk):
            dst_ref[pl.ds(pl.multiple_of(c * WCHUNK, WCHUNK), WCHUNK), :] = _bf16(chunk)
        return consume

    _stream_chunks(COLS_A // WCHUNK, in_proj_rows(0), transposed_into(w_a_in_ref),
                   stage_ref, sem_ref)
    _stream_chunks(COLS_B // WCHUNK, in_proj_rows(COLS_A + GATE_RANK),
                   transposed_into(w_b_in_ref), stage_ref, sem_ref)
    for hbm, dst in ((wa_hbm, w_a_ref), (wc_hbm, w_c_ref), (wo_hbm, w_out_ref),
                     (wpg_hbm, w_pg_ref), (wpe_hbm, w_pe_ref)):
        _stream_chunks(dst.shape[0] // WCHUNK,
                       lambda c, hbm=hbm: hbm.at[layer, pl.ds(c * WCHUNK, WCHUNK), :],
                       rows_into(dst), stage_ref, sem_ref)
    gate_rows = jnp.concatenate(
        [gate_rows_ref[...], jnp.zeros((LANE - GATE_RANK, D_MODEL), F32)], axis=0)
    w_gkl_ref[...] = _bf16(gate_rows.T)
    w_gk2_ref[...] = jnp.zeros_like(w_gk2_ref)
    w_gk2_ref[:GATE_RANK, :] = _bf16(w_gk2_f32_ref[...])


def _tile_stages(s, x_ref, p_ref, w_a_in_ref, w_gkl_ref, w_b_in_ref, w_gk2_ref, w_a_ref,
                 w_c_ref, w_out_ref, w_pg_ref, w_pe_ref, g_mix_ref, b_gk_ref, g_gla_ref,
                 conv_w_ref, b_merge_ref, g_ple_ref, g_fin_ref, out_ref, state_ref, carry_ref,
                 apply_final_norm):
    ts = x_ref.shape[1]
    n_chunks = ts // GLA_CHUNK

    x = x_ref[s]
    h = _bf16(_rms(x, g_mix_ref[...]))
    yield

    def proj(w_ref, off, width):
        return jnp.concatenate(
            [jnp.dot(h, w_ref[c], preferred_element_type=F32)
             for c in range(off // WCHUNK, (off + width) // WCHUNK)], axis=1)

    proj_a = functools.partial(proj, w_a_in_ref)
    proj_b = functools.partial(proj, w_b_in_ref)

    gk_low = _bf16(jnp.dot(h, w_gkl_ref[...], preferred_element_type=F32))
    q = proj_a(OFF_Q, GLA_KW)
    k = proj_a(OFF_K, GLA_KW)
    yield
    pre = jnp.dot(gk_low, w_gk2_ref[...], preferred_element_type=F32) + b_gk_ref[...]
    v = _bf16(proj_a(OFF_V, GLA_VW))
    yield
    gk = _log_sigmoid(pre) * (1.0 / GATE_NORMALIZER)
    gk_hi = _bf16(gk)
    gk_lo = _bf16(gk - gk_hi.astype(F32))
    z_a = proj_a(OFF_ZA, GLA_VW)
    act_a = _bf16(z_a * _sigmoid(z_a))
    yield
    row = lax.broadcasted_iota(jnp.int32, (ts, ts), 0)
    col = lax.broadcasted_iota(jnp.int32, (ts, ts), 1)
    causal = col <= row
    tri = jnp.where(causal, 1.0, 0.0).astype(jnp.bfloat16)
    cum = (jnp.dot(tri, gk_hi, preferred_element_type=F32)
           + jnp.dot(tri, gk_lo, preferred_element_type=F32))
    m = proj_b(OFF_CC, CONV_WIDTH) * proj_b(OFF_XC, CONV_WIDTH)
    yield
    cum_end = cum[ts - 1:ts, :]
    starts = [jnp.zeros((1, GLA_KW), F32)]
    starts += [cum[c * GLA_CHUNK - 1:c * GLA_CHUNK, :] for c in range(1, n_chunks)]
    start_full = jnp.concatenate(
        [jnp.broadcast_to(st, (GLA_CHUNK, GLA_KW)) for st in starts], axis=0)
    exp_start_full = jnp.concatenate(
        [jnp.broadcast_to(jnp.exp(st), (GLA_CHUNK, GLA_KW)) for st in starts], axis=0)
    q_loc = q * (jnp.exp(cum - start_full) * (GLA_DK ** -0.5))
    q_in = _bf16(q_loc)
    q_tile = _bf16(q_loc * exp_start_full)
    k_end = _bf16(k * jnp.exp(cum_end - cum))
    k_rel = []
    for c in range(n_chunks):
        n = _round_up((c + 1) * GLA_CHUNK, LANE)
        k_rel.append(_bf16(k[:n, :] * jnp.exp(starts[c] - cum[:n, :])))
    tile_decay = jnp.exp(cum_end)
    ext = jnp.concatenate([carry_ref[s], m], axis=0)
    carry_ref[s] = m[ts - SUBLANE:, :]
    m1 = pltpu.roll(ext, 1, 0)[SUBLANE:, :]
    m2 = pltpu.roll(ext, 2, 0)[SUBLANE:, :]
    cw = conv_w_ref[...]
    u = proj_b(OFF_CB, CONV_WIDTH) * (cw[0:1, :] * m2 + cw[1:2, :] * m1 + cw[2:3, :] * m)
    yield

    attn, s_old = [], []
    for hd in range(GLA_HEADS):
        ks = slice(hd * GLA_DK, (hd + 1) * GLA_DK)
        vs = slice(hd * GLA_DV, (hd + 1) * GLA_DV)
        s_t = state_ref[s, hd]
        s_old.append(_bf16(s_t))
        blocks = []
        for c in range(n_chunks):
            rs = slice(c * GLA_CHUNK, (c + 1) * GLA_CHUNK)
            a_c = lax.dot_general(q_in[rs, ks], k_rel[c][:, ks], _NT,
                                  preferred_element_type=F32)
            if a_c.shape[1] < ts:
                a_c = jnp.concatenate(
                    [a_c, jnp.zeros((GLA_CHUNK, ts - a_c.shape[1]), F32)], axis=1)
            blocks.append(a_c)
        attn.append(_bf16(jnp.where(causal, jnp.concatenate(blocks, axis=0), 0.0)))
        state_ref[s, hd] = s_t * tile_decay[:, ks] + lax.dot_general(
            v[:, vs], k_end[:, ks], _TN, preferred_element_type=F32)
    z_c = proj_b(OFF_ZC, CONV_WIDTH)
    gated_c = _bf16(u * (z_c * _sigmoid(z_c)))
    yield
    g_gla = g_gla_ref[...]
    gated = []
    for hd in range(GLA_HEADS):
        ks = slice(hd * GLA_DK, (hd + 1) * GLA_DK)
        vs = slice(hd * GLA_DV, (hd + 1) * GLA_DV)
        o = (lax.dot_general(q_tile[:, ks], s_old[hd], _NT, preferred_element_type=F32)
             + jnp.dot(attn[hd], v[:, vs], preferred_element_type=F32))
        gated.append(_bf16(_rms(o, g_gla) * act_a[:, vs]))
    b_merge = b_merge_ref[...]
    gate_c = _bf16(_sigmoid(proj_b(OFF_GC, D_MODEL) + b_merge[:, D_MODEL:]))
    yield
    y_c = _bf16(gate_c * jnp.dot(gated_c, w_c_ref[...], preferred_element_type=F32))
    gate_a = _bf16(_sigmoid(proj_b(OFF_GA, D_MODEL) + b_merge[:, :D_MODEL]))
    yield
    y_a = jnp.dot(jnp.concatenate(gated, axis=1), w_a_ref[...], preferred_element_type=F32)
    merged = _bf16(gate_a * y_a + y_c)
    yield

    x1 = x + jnp.dot(merged, w_out_ref[...], preferred_element_type=F32)
    yield
    hn = _bf16(_rms(x1, g_ple_ref[...]))
    yield
    ple_gate = _sigmoid(jnp.dot(hn, w_pg_ref[...], preferred_element_type=F32))
    x2 = x1 + ple_gate * jnp.dot(_bf16(p_ref[s]), w_pe_ref[...], preferred_element_type=F32)
    yield
    if apply_final_norm:
        x2 = _rms(x2, g_fin_ref[...])
    out_ref[s] = x2


def _layer_kernel(x_ref, p_ref, w_in_t_hbm, wa_hbm, wc_hbm, wo_hbm, wpg_hbm, wpe_hbm,
                  gate_rows_ref, w_gk2_f32_ref, g_mix_ref, b_gk_ref, g_gla_ref, conv_w_ref,
                  b_merge_ref, g_ple_ref, g_fin_ref, out_ref,
                  w_a_in_ref, w_gkl_ref, w_b_in_ref, w_gk2_ref, w_a_ref, w_c_ref, w_out_ref,
                  w_pg_ref, w_pe_ref, stage_ref, sem_ref, state_ref, carry_ref,
                  *, layer, apply_final_norm):
    @pl.when((pl.program_id(0) == 0) & (pl.program_id(1) == 0))
    def _():
        _load_weights(layer, w_in_t_hbm, wa_hbm, wc_hbm, wo_hbm, wpg_hbm, wpe_hbm, gate_rows_ref,
                      w_gk2_f32_ref, w_a_in_ref, w_gkl_ref, w_b_in_ref, w_gk2_ref, w_a_ref,
                      w_c_ref, w_out_ref, w_pg_ref, w_pe_ref, stage_ref, sem_ref)

    @pl.when(pl.program_id(1) == 0)
    def _():
        state_ref[...] = jnp.zeros_like(state_ref)
        carry_ref[...] = jnp.zeros_like(carry_ref)

    live = [_tile_stages(s, x_ref, p_ref, w_a_in_ref, w_gkl_ref, w_b_in_ref, w_gk2_ref, w_a_ref,
                         w_c_ref, w_out_ref, w_pg_ref, w_pe_ref, g_mix_ref, b_gk_ref, g_gla_ref,
                         conv_w_ref, b_merge_ref, g_ple_ref, g_fin_ref, out_ref, state_ref,
                         carry_ref, apply_final_norm)
            for s in range(x_ref.shape[0])]
    while live:
        for g in list(live):
            try:
                next(g)
            except StopIteration:
                live.remove(g)


def _const_spec(shape, index=None):
    index = (0,) * len(shape) if index is None else index
    return pl.BlockSpec(shape, lambda b, t: index, pipeline_mode=pl.Buffered(1))


def _layer(layer, x, p_i, w_in, w_gk2, big_weights, vectors, *, apply_final_norm):
    bsz, seq, d = x.shape
    assert d == D_MODEL and seq % SEQ_TILE == 0 and SEQ_TILE % LANE == 0
    assert w_in.shape[1:] == (D_MODEL, IN_COLS) and COLS_A % GATE_RANK == 0
    nb = 2 if bsz % 2 == 0 else 1
    w_in_t = jnp.swapaxes(w_in, 1, 2)
    tile = lambda width: pl.BlockSpec((nb, SEQ_TILE, width), lambda b, t: (b, t, 0))
    in_hbm = pl.BlockSpec(memory_space=pltpu.HBM)
    in_specs = ([tile(D_MODEL), tile(PLE_DIM)] + [in_hbm] * (1 + len(big_weights))
                + [_const_spec((None, GATE_RANK, D_MODEL), (layer, COLS_A // GATE_RANK, 0)),
                   _const_spec((None, GATE_RANK, GLA_KW), (layer, 0, 0))]
                + [_const_spec(vv.shape) for vv in vectors])
    bf16 = jnp.bfloat16
    return pl.pallas_call(
        functools.partial(_layer_kernel, layer=layer, apply_final_norm=apply_final_norm),
        grid=(bsz // nb, seq // SEQ_TILE),
        in_specs=in_specs,
        out_specs=tile(D_MODEL),
        out_shape=jax.ShapeDtypeStruct(x.shape, jnp.float32),
        scratch_shapes=[
            pltpu.VMEM((COLS_A // WCHUNK, D_MODEL, WCHUNK), bf16),
            pltpu.VMEM((D_MODEL, LANE), bf16),
            pltpu.VMEM((COLS_B // WCHUNK, D_MODEL, WCHUNK), bf16),
            pltpu.VMEM((LANE, GLA_KW), bf16),
            pltpu.VMEM((GLA_VW, D_MODEL), bf16),
            pltpu.VMEM((CONV_WIDTH, D_MODEL), bf16),
            pltpu.VMEM((D_MODEL, D_MODEL), bf16),
            pltpu.VMEM((D_MODEL, D_MODEL), bf16),
            pltpu.VMEM((PLE_DIM, D_MODEL), bf16),
            pltpu.VMEM((WSLOTS, WCHUNK, D_MODEL), jnp.float32),
            pltpu.SemaphoreType.DMA((WSLOTS,)),
            pltpu.VMEM((nb, GLA_HEADS, GLA_DV, GLA_DK), jnp.float32),
            pltpu.VMEM((nb, SUBLANE, CONV_WIDTH), jnp.float32),
        ],
        compiler_params=pltpu.CompilerParams(
            dimension_semantics=("arbitrary", "arbitrary"),
            vmem_limit_bytes=VMEM_LIMIT_BYTES),
        name="hybrid_layer",
    )(x, p_i, w_in_t, *big_weights, w_in_t, w_gk2, *vectors)


def kernel(x, p, norm_mix_g, w_in, b_merge, w_gk2, b_gk, gla_norm_g, conv_w, w_branch_a,
           w_branch_c, w_out, norm_ple_g, w_ple_gate, w_ple_proj, norm_final_g):
    depth = w_in.shape[0]
    row = lambda a: a.reshape(1, -1).astype(jnp.float32)
    for i in range(depth):
        vectors = [row(norm_mix_g[i]), row(b_gk[i]), row(gla_norm_g[i]),
                   conv_w[i].astype(jnp.float32), row(b_merge[i]), row(norm_ple_g[i]),
                   row(norm_final_g)]
        x = _layer(i, x, p[i], w_in, w_gk2,
                   [w_branch_a, w_branch_c, w_out, w_ple_gate, w_ple_proj], vectors,
                   apply_final_norm=(i == depth - 1))
    return x
```

```python
import functools

import jax
import jax.numpy as jnp
from jax import lax
from jax.experimental import pallas as pl
from jax.experimental.pallas import tpu as pltpu

D_MODEL = 1024
GLA_HEADS = 4
GLA_DK = 128
GLA_DV = 256
GLA_KW = GLA_HEADS * GLA_DK
GLA_VW = GLA_HEADS * GLA_DV
GATE_RANK = 16
GATE_NORMALIZER = 16.0
GLA_CHUNK = 64
CONV_WIDTH = 1024
CONV_K = 3
PLE_DIM = 256
EPS = 1e-6

LANE = 128
SUBLANE = 8
SEQ_TILE = 256
WCHUNK = 256
WSLOTS = 4
VMEM_LIMIT_BYTES = 58 * 1024 * 1024

OFF_Q = 0
OFF_K = OFF_Q + GLA_KW
OFF_V = OFF_K + GLA_KW
OFF_ZA = OFF_V + GLA_VW
COLS_A = OFF_ZA + GLA_VW
OFF_CB = 0
OFF_CC = OFF_CB + CONV_WIDTH
OFF_XC = OFF_CC + CONV_WIDTH
OFF_ZC = OFF_XC + CONV_WIDTH
OFF_GA = OFF_ZC + CONV_WIDTH
OFF_GC = OFF_GA + D_MODEL
COLS_B = OFF_GC + D_MODEL
IN_COLS = COLS_A + GATE_RANK + COLS_B

_NT = (((1,), (1,)), ((), ()))
_TN = (((0,), (0,)), ((), ()))
F32 = jnp.float32


def _bf16(a):
    return a.astype(jnp.bfloat16)


def _rms(xf, g):
    ms = jnp.mean(xf * xf, axis=-1, keepdims=True)
    return xf * lax.rsqrt(ms + EPS) * g


def _sigmoid(a):
    return 0.5 * jnp.tanh(0.5 * a) + 0.5


def _log_sigmoid(a):
    return jnp.minimum(a, 0.0) - jnp.log(1.0 + jnp.exp(-jnp.abs(a)))


def _round_up(n, m):
    return (n + m - 1) // m * m


def _stream_chunks(n_chunks, src_chunk, consume, stage_ref, sem_ref):
    ahead = WSLOTS - 1

    def copy(c):
        slot = c % WSLOTS
        return pltpu.make_async_copy(src_chunk(c), stage_ref.at[slot], sem_ref.at[slot])

    for c in range(min(ahead, n_chunks)):
        copy(c).start()

    def body(c, carry):
        copy(c).wait()

        @pl.when(c + ahead < n_chunks)
        def _():
            copy(c + ahead).start()

        consume(c, stage_ref[c % WSLOTS])
        return carry

    lax.fori_loop(0, n_chunks, body, 0)


def _load_weights(layer, w_in_t_hbm, wa_hbm, wc_hbm, wo_hbm, wpg_hbm, wpe_hbm, gate_rows_ref,
                  w_gk2_f32_ref, w_a_in_ref, w_gkl_ref, w_b_in_ref, w_gk2_ref, w_a_ref, w_c_ref,
                  w_out_ref, w_pg_ref, w_pe_ref, stage_ref, sem_ref):
    def in_proj_rows(first_row):
        return lambda c: w_in_t_hbm.at[layer, pl.ds(first_row + c * WCHUNK, WCHUNK), :]

    def transposed_into(dst_ref):
        def consume(c, chunk):
            dst_ref[c] = _bf16(chunk.T)
        return consume

    def rows_into(dst_ref):
        def consume(c, chunk):
            dst_ref[pl.ds(pl.multiple_of(c * WCHUNK, WCHUNK), WCHUNK), :] = _bf16(chunk)
        return consume

    _stream_chunks(COLS_A // WCHUNK, in_proj_rows(0), transposed_into(w_a_in_ref),
                   stage_ref, sem_ref)
    _stream_chunks(COLS_B // WCHUNK, in_proj_rows(COLS_A + GATE_RANK),
                   transposed_into(w_b_in_ref), stage_ref, sem_ref)
    for hbm, dst in ((wa_hbm, w_a_ref), (wc_hbm, w_c_ref), (wo_hbm, w_out_ref),
                     (wpg_hbm, w_pg_ref), (wpe_hbm, w_pe_ref)):
        _stream_chunks(dst.shape[0] // WCHUNK,
                       lambda c, hbm=hbm: hbm.at[layer, pl.ds(c * WCHUNK, WCHUNK), :],
                       rows_into(dst), stage_ref, sem_ref)
    gate_rows = jnp.concatenate(
        [gate_rows_ref[...], jnp.zeros((LANE - GATE_RANK, D_MODEL), F32)], axis=0)
    w_gkl_ref[...] = _bf16(gate_rows.T)
    w_gk2_ref[...] = jnp.zeros_like(w_gk2_ref)
    w_gk2_ref[:GATE_RANK, :] = _bf16(w_gk2_f32_ref[...])


def _tile_stages(s, x_ref, p_ref, w_a_in_ref, w_gkl_ref, w_b_in_ref, w_gk2_ref, w_a_ref,
                 w_c_ref, w_out_ref, w_pg_ref, w_pe_ref, g_mix_ref, b_gk_ref, g_gla_ref,
                 conv_w_ref, b_merge_ref, g_ple_ref, g_fin_ref, out_ref, state_ref, carry_ref,
                 apply_final_norm):
    ts = x_ref.shape[1]
    n_chunks = ts // GLA_CHUNK

    x = x_ref[s]
    h = _bf16(_rms(x, g_mix_ref[...]))
    yield

    def proj(w_ref, off, width):
        return jnp.concatenate(
            [jnp.dot(h, w_ref[c], preferred_element_type=F32)
             for c in range(off // WCHUNK, (off + width) // WCHUNK)], axis=1)

    proj_a = functools.partial(proj, w_a_in_ref)
    proj_b = functools.partial(proj, w_b_in_ref)

    gk_low = _bf16(jnp.dot(h, w_gkl_ref[...], preferred_element_type=F32))
    q = proj_a(OFF_Q, GLA_KW)
    k = proj_a(OFF_K, GLA_KW)
    yield
    pre = jnp.dot(gk_low, w_gk2_ref[...], preferred_element_type=F32) + b_gk_ref[...]
    v = _bf16(proj_a(OFF_V, GLA_VW))
    yield
    z_a = proj_a(OFF_ZA, GLA_VW)
    gk = _log_sigmoid(pre) * (1.0 / GATE_NORMALIZER)
    act_a = _bf16(z_a * _sigmoid(z_a))
    yield
    m = proj_b(OFF_CC, CONV_WIDTH) * proj_b(OFF_XC, CONV_WIDTH)
    cum = gk
    shift = 1
    while shift < ts:
        cum = cum + jnp.concatenate(
            [jnp.zeros((shift, GLA_KW), F32), cum[:ts - shift, :]], axis=0)
        shift *= 2
    row = lax.broadcasted_iota(jnp.int32, (ts, ts), 0)
    col = lax.broadcasted_iota(jnp.int32, (ts, ts), 1)
    causal = col <= row
    yield
    cb = proj_b(OFF_CB, CONV_WIDTH)
    cum_end = cum[ts - 1:ts, :]
    starts = [jnp.zeros((1, GLA_KW), F32)]
    starts += [cum[c * GLA_CHUNK - 1:c * GLA_CHUNK, :] for c in range(1, n_chunks)]
    start_full = jnp.concatenate(
        [jnp.broadcast_to(st, (GLA_CHUNK, GLA_KW)) for st in starts], axis=0)
    exp_start_full = jnp.concatenate(
        [jnp.broadcast_to(jnp.exp(st), (GLA_CHUNK, GLA_KW)) for st in starts], axis=0)
    q_loc = q * (jnp.exp(cum - start_full) * (GLA_DK ** -0.5))
    q_in = _bf16(q_loc)
    q_tile = _bf16(q_loc * exp_start_full)
    k_end = _bf16(k * jnp.exp(cum_end - cum))
    k_rel = []
    for c in range(n_chunks):
        n = _round_up((c + 1) * GLA_CHUNK, LANE)
        k_rel.append(_bf16(k[:n, :] * jnp.exp(starts[c] - cum[:n, :])))
    tile_decay = jnp.exp(cum_end)
    ext = jnp.concatenate([carry_ref[s], m], axis=0)
    carry_ref[s] = m[ts - SUBLANE:, :]
    m1 = pltpu.roll(ext, 1, 0)[SUBLANE:, :]
    m2 = pltpu.roll(ext, 2, 0)[SUBLANE:, :]
    cw = conv_w_ref[...]
    u = cb * (cw[0:1, :] * m2 + cw[1:2, :] * m1 + cw[2:3, :] * m)
    yield

    attn, s_old = [], []
    for hd in range(GLA_HEADS):
        ks = slice(hd * GLA_DK, (hd + 1) * GLA_DK)
        vs = slice(hd * GLA_DV, (hd + 1) * GLA_DV)
        s_t = state_ref[s, hd]
        s_old.append(_bf16(s_t))
        blocks = []
        for c in range(n_chunks):
            rs = slice(c * GLA_CHUNK, (c + 1) * GLA_CHUNK)
            a_c = lax.dot_general(q_in[rs, ks], k_rel[c][:, ks], _NT,
                                  preferred_element_type=F32)
            if a_c.shape[1] < ts:
                a_c = jnp.concatenate(
                    [a_c, jnp.zeros((GLA_CHUNK, ts - a_c.shape[1]), F32)], axis=1)
            blocks.append(a_c)
        attn.append(_bf16(jnp.where(causal, jnp.concatenate(blocks, axis=0), 0.0)))
        state_ref[s, hd] = s_t * tile_decay[:, ks] + lax.dot_general(
            v[:, vs], k_end[:, ks], _TN, preferred_element_type=F32)
    z_c = proj_b(OFF_ZC, CONV_WIDTH)
    gated_c = _bf16(u * (z_c * _sigmoid(z_c)))
    yield
    g_gla = g_gla_ref[...]
    gated = []
    for hd in range(GLA_HEADS):
        ks = slice(hd * GLA_DK, (hd + 1) * GLA_DK)
        vs = slice(hd * GLA_DV, (hd + 1) * GLA_DV)
        o = (lax.dot_general(q_tile[:, ks], s_old[hd], _NT, preferred_element_type=F32)
             + jnp.dot(attn[hd], v[:, vs], preferred_element_type=F32))
        gated.append(_bf16(_rms(o, g_gla) * act_a[:, vs]))
    b_merge = b_merge_ref[...]
    gate_c = _bf16(_sigmoid(proj_b(OFF_GC, D_MODEL) + b_merge[:, D_MODEL:]))
    yield
    y_c = _bf16(gate_c * jnp.dot(gated_c, w_c_ref[...], preferred_element_type=F32))
    gate_a = _bf16(_sigmoid(proj_b(OFF_GA, D_MODEL) + b_merge[:, :D_MODEL]))
    yield
    y_a = jnp.dot(jnp.concatenate(gated, axis=1), w_a_ref[...], preferred_element_type=F32)
    merged = _bf16(gate_a * y_a + y_c)
    yield

    x1 = x + jnp.dot(merged, w_out_ref[...], preferred_element_type=F32)
    yield
    hn = _bf16(_rms(x1, g_ple_ref[...]))
    yield
    ple_gate = _sigmoid(jnp.dot(hn, w_pg_ref[...], preferred_element_type=F32))
    x2 = x1 + ple_gate * jnp.dot(_bf16(p_ref[s]), w_pe_ref[...], preferred_element_type=F32)
    yield
    if apply_final_norm:
        x2 = _rms(x2, g_fin_ref[...])
    out_ref[s] = x2


def _layer_kernel(x_ref, p_ref, w_in_t_hbm, wa_hbm, wc_hbm, wo_hbm, wpg_hbm, wpe_hbm,
                  gate_rows_ref, w_gk2_f32_ref, g_mix_ref, b_gk_ref, g_gla_ref, conv_w_ref,
                  b_merge_ref, g_ple_ref, g_fin_ref, out_ref,
                  w_a_in_ref, w_gkl_ref, w_b_in_ref, w_gk2_ref, w_a_ref, w_c_ref, w_out_ref,
                  w_pg_ref, w_pe_ref, stage_ref, sem_ref, state_ref, carry_ref,
                  *, layer, apply_final_norm):
    @pl.when((pl.program_id(0) == 0) & (pl.program_id(1) == 0))
    def _():
        _load_weights(layer, w_in_t_hbm, wa_hbm, wc_hbm, wo_hbm, wpg_hbm, wpe_hbm, gate_rows_ref,
                      w_gk2_f32_ref, w_a_in_ref, w_gkl_ref, w_b_in_ref, w_gk2_ref, w_a_ref,
                      w_c_ref, w_out_ref, w_pg_ref, w_pe_ref, stage_ref, sem_ref)

    @pl.when(pl.program_id(1) == 0)
    def _():
        state_ref[...] = jnp.zeros_like(state_ref)
        carry_ref[...] = jnp.zeros_like(carry_ref)

    live = [_tile_stages(s, x_ref, p_ref, w_a_in_ref, w_gkl_ref, w_b_in_ref, w_gk2_ref, w_a_ref,
                         w_c_ref, w_out_ref, w_pg_ref, w_pe_ref, g_mix_ref, b_gk_ref, g_gla_ref,
                         conv_w_ref, b_merge_ref, g_ple_ref, g_fin_ref, out_ref, state_ref,
                         carry_ref, apply_final_norm)
            for s in range(x_ref.shape[0])]
    while live:
        for g in list(live):
            try:
                next(g)
            except StopIteration:
                live.remove(g)


def _const_spec(shape, index=None):
    index = (0,) * len(shape) if index is None else index
    return pl.BlockSpec(shape, lambda b, t: index, pipeline_mode=pl.Buffered(1))


def _layer(layer, x, p_i, w_in, w_gk2, big_weights, vectors, *, apply_final_norm):
    bsz, seq, d = x.shape
    assert d == D_MODEL and seq % SEQ_TILE == 0 and SEQ_TILE % LANE == 0
    assert w_in.shape[1:] == (D_MODEL, IN_COLS) and COLS_A % GATE_RANK == 0
    nb = 2 if bsz % 2 == 0 else 1
    w_in_t = jnp.swapaxes(w_in, 1, 2)
    tile = lambda width: pl.BlockSpec((nb, SEQ_TILE, width), lambda b, t: (b, t, 0))
    in_hbm = pl.BlockSpec(memory_space=pltpu.HBM)
    in_specs = ([tile(D_MODEL), tile(PLE_DIM)] + [in_hbm] * (1 + len(big_weights))
                + [_const_spec((None, GATE_RANK, D_MODEL), (layer, COLS_A // GATE_RANK, 0)),
                   _const_spec((None, GATE_RANK, GLA_KW), (layer, 0, 0))]
                + [_const_spec(vv.shape) if vv.ndim == 2
                   else _const_spec((None,) + vv.shape[1:], (layer, 0, 0)) for vv in vectors])
    bf16 = jnp.bfloat16
    return pl.pallas_call(
        functools.partial(_layer_kernel, layer=layer, apply_final_norm=apply_final_norm),
        grid=(bsz // nb, seq // SEQ_TILE),
        in_specs=in_specs,
        out_specs=tile(D_MODEL),
        out_shape=jax.ShapeDtypeStruct(x.shape, jnp.float32),
        scratch_shapes=[
            pltpu.VMEM((COLS_A // WCHUNK, D_MODEL, WCHUNK), bf16),
            pltpu.VMEM((D_MODEL, LANE), bf16),
            pltpu.VMEM((COLS_B // WCHUNK, D_MODEL, WCHUNK), bf16),
            pltpu.VMEM((LANE, GLA_KW), bf16),
            pltpu.VMEM((GLA_VW, D_MODEL), bf16),
            pltpu.VMEM((CONV_WIDTH, D_MODEL), bf16),
            pltpu.VMEM((D_MODEL, D_MODEL), bf16),
            pltpu.VMEM((D_MODEL, D_MODEL), bf16),
            pltpu.VMEM((PLE_DIM, D_MODEL), bf16),
            pltpu.VMEM((WSLOTS, WCHUNK, D_MODEL), jnp.float32),
            pltpu.SemaphoreType.DMA((WSLOTS,)),
            pltpu.VMEM((nb, GLA_HEADS, GLA_DV, GLA_DK), jnp.float32),
            pltpu.VMEM((nb, SUBLANE, CONV_WIDTH), jnp.float32),
        ],
        compiler_params=pltpu.CompilerParams(
            dimension_semantics=("arbitrary", "arbitrary"),
            vmem_limit_bytes=VMEM_LIMIT_BYTES),
        name="hybrid_layer",
    )(x, p_i, w_in_t, *big_weights, w_in_t, w_gk2, *vectors)


def kernel(x, p, norm_mix_g, w_in, b_merge, w_gk2, b_gk, gla_norm_g, conv_w, w_branch_a,
           w_branch_c, w_out, norm_ple_g, w_ple_gate, w_ple_proj, norm_final_g):
    depth = w_in.shape[0]
    row = lambda a: a.reshape(1, -1).astype(jnp.float32)
    for i in range(depth):
        vectors = [row(norm_mix_g[i]), row(b_gk[i]), row(gla_norm_g[i]),
                   conv_w.astype(jnp.float32), row(b_merge[i]), row(norm_ple_g[i]),
                   row(norm_final_g)]
        x = _layer(i, x, p[i], w_in, w_gk2,
                   [w_branch_a, w_branch_c, w_out, w_ple_gate, w_ple_proj], vectors,
                   apply_final_norm=(i == depth - 1))
    return x
```

```python
import functools

import jax
import jax.numpy as jnp
from jax import lax
from jax.experimental import pallas as pl
from jax.experimental.pallas import tpu as pltpu

D_MODEL = 1024
GLA_HEADS = 4
GLA_DK = 128
GLA_DV = 256
GLA_KW = GLA_HEADS * GLA_DK
GLA_VW = GLA_HEADS * GLA_DV
GATE_RANK = 16
GATE_NORMALIZER = 16.0
GLA_CHUNK = 64
CONV_WIDTH = 1024
CONV_K = 3
PLE_DIM = 256
EPS = 1e-6

LANE = 128
SUBLANE = 8
SEQ_TILE = 256
WCHUNK = 256
WSLOTS = 4
VMEM_LIMIT_BYTES = 58 * 1024 * 1024

OFF_Q = 0
OFF_K = OFF_Q + GLA_KW
OFF_V = OFF_K + GLA_KW
OFF_ZA = OFF_V + GLA_VW
COLS_A = OFF_ZA + GLA_VW
OFF_CB = 0
OFF_CC = OFF_CB + CONV_WIDTH
OFF_XC = OFF_CC + CONV_WIDTH
OFF_ZC = OFF_XC + CONV_WIDTH
OFF_GA = OFF_ZC + CONV_WIDTH
OFF_GC = OFF_GA + D_MODEL
COLS_B = OFF_GC + D_MODEL
IN_COLS = COLS_A + GATE_RANK + COLS_B

_NT = (((1,), (1,)), ((), ()))
_TN = (((0,), (0,)), ((), ()))
F32 = jnp.float32


def _bf16(a):
    return a.astype(jnp.bfloat16)


def _rms(xf, g):
    ms = jnp.mean(xf * xf, axis=-1, keepdims=True)
    return xf * lax.rsqrt(ms + EPS) * g


def _sigmoid(a):
    return 0.5 * jnp.tanh(0.5 * a) + 0.5


def _log_sigmoid(a):
    return jnp.minimum(a, 0.0) - jnp.log(1.0 + jnp.exp(-jnp.abs(a)))


def _round_up(n, m):
    return (n + m - 1) // m * m


def _stream_chunks(n_chunks, src_chunk, consume, stage_ref, sem_ref):
    ahead = WSLOTS - 1

    def copy(c):
        slot = c % WSLOTS
        return pltpu.make_async_copy(src_chunk(c), stage_ref.at[slot], sem_ref.at[slot])

    for c in range(min(ahead, n_chunks)):
        copy(c).start()

    def body(c, carry):
        copy(c).wait()

        @pl.when(c + ahead < n_chunks)
        def _():
            copy(c + ahead).start()

        consume(c, stage_ref[c % WSLOTS])
        return carry

    lax.fori_loop(0, n_chunks, body, 0)


def _load_weights(layer, w_in_t_hbm, wa_hbm, wc_hbm, wo_hbm, wpg_hbm, wpe_hbm, gate_rows_ref,
                  w_gk2_f32_ref, w_a_in_ref, w_gkl_ref, w_b_in_ref, w_gk2_ref, w_a_ref, w_c_ref,
                  w_out_ref, w_pg_ref, w_pe_ref, stage_ref, sem_ref):
    def in_proj_rows(first_row):
        return lambda c: w_in_t_hbm.at[layer, pl.ds(first_row + c * WCHUNK, WCHUNK), :]

    def transposed_into(dst_ref):
        def consume(c, chunk):
            dst_ref[c] = _bf16(chunk.T)
        return consume

    def rows_into(dst_ref):
        def consume(c, chunk):
            dst_ref[pl.ds(pl.multiple_of(c * WCHUNK, WCHUNK), WCHUNK), :] = _bf16(chunk)
        return consume

    _stream_chunks(COLS_A // WCHUNK, in_proj_rows(0), transposed_into(w_a_in_ref),
                   stage_ref, sem_ref)
    _stream_chunks(COLS_B // WCHUNK, in_proj_rows(COLS_A + GATE_RANK),
                   transposed_into(w_b_in_ref), stage_ref, sem_ref)
    for hbm, dst in ((wa_hbm, w_a_ref), (wc_hbm, w_c_ref), (wo_hbm, w_out_ref),
                     (wpg_hbm, w_pg_ref), (wpe_hbm, w_pe_ref)):
        _stream_chunks(dst.shape[0] // WCHUNK,
                       lambda c, hbm=hbm: hbm.at[layer, pl.ds(c * WCHUNK, WCHUNK), :],
                       rows_into(dst), stage_ref, sem_ref)
    gate_rows = jnp.concatenate(
        [gate_rows_ref[...], jnp.zeros((LANE - GATE_RANK, D_MODEL), F32)], axis=0)
    w_gkl_ref[...] = _bf16(gate_rows.T)
    w_gk2_ref[...] = jnp.zeros_like(w_gk2_ref)
    w_gk2_ref[:GATE_RANK, :] = _bf16(w_gk2_f32_ref[...])


def _tile_stages(s, x_ref, p_ref, w_a_in_ref, w_gkl_ref, w_b_in_ref, w_gk2_ref, w_a_ref,
                 w_c_ref, w_out_ref, w_pg_ref, w_pe_ref, g_mix_ref, b_gk_ref, g_gla_ref,
                 conv_w_ref, b_merge_ref, g_ple_ref, g_fin_ref, out_ref, state_ref, carry_ref,
                 apply_final_norm):
    ts = x_ref.shape[1]
    n_chunks = ts // GLA_CHUNK

    x = x_ref[s]
    h = _bf16(_rms(x, g_mix_ref[...]))
    yield

    def proj(w_ref, off, width):
        return jnp.concatenate(
            [jnp.dot(h, w_ref[c], preferred_element_type=F32)
             for c in range(off // WCHUNK, (off + width) // WCHUNK)], axis=1)

    proj_a = functools.partial(proj, w_a_in_ref)
    proj_b = functools.partial(proj, w_b_in_ref)

    gk_low = _bf16(jnp.dot(h, w_gkl_ref[...], preferred_element_type=F32))
    q = proj_a(OFF_Q, GLA_KW)
    k = proj_a(OFF_K, GLA_KW)
    yield
    pre = jnp.dot(gk_low, w_gk2_ref[...], preferred_element_type=F32) + b_gk_ref[...]
    v = _bf16(proj_a(OFF_V, GLA_VW))
    yield
    z_a = proj_a(OFF_ZA, GLA_VW)
    gk = _log_sigmoid(pre) * (1.0 / GATE_NORMALIZER)
    act_a = _bf16(z_a * _sigmoid(z_a))
    yield
    m = proj_b(OFF_CC, CONV_WIDTH) * proj_b(OFF_XC, CONV_WIDTH)
    cum = gk
    shift = 1
    while shift < ts:
        cum = cum + jnp.concatenate(
            [jnp.zeros((shift, GLA_KW), F32), cum[:ts - shift, :]], axis=0)
        shift *= 2
    row = lax.broadcasted_iota(jnp.int32, (ts, ts), 0)
    col = lax.broadcasted_iota(jnp.int32, (ts, ts), 1)
    causal = col <= row
    yield
    cb = proj_b(OFF_CB, CONV_WIDTH)
    cum_end = cum[ts - 1:ts, :]
    starts = [jnp.zeros((1, GLA_KW), F32)]
    starts += [cum[c * GLA_CHUNK - 1:c * GLA_CHUNK, :] for c in range(1, n_chunks)]
    start_full = jnp.concatenate(
        [jnp.broadcast_to(st, (GLA_CHUNK, GLA_KW)) for st in starts], axis=0)
    exp_start_full = jnp.concatenate(
        [jnp.broadcast_to(jnp.exp(st), (GLA_CHUNK, GLA_KW)) for st in starts], axis=0)
    q_loc = q * (jnp.exp(cum - start_full) * (GLA_DK ** -0.5))
    q_in = _bf16(q_loc)
    q_tile = _bf16(q_loc * exp_start_full)
    k_end = _bf16(k * jnp.exp(cum_end - cum))
    k_rel = []
    for c in range(n_chunks):
        n = _round_up((c + 1) * GLA_CHUNK, LANE)
        k_rel.append(_bf16(k[:n, :] * jnp.exp(starts[c] - cum[:n, :])))
    eye = (lax.broadcasted_iota(jnp.int32, (GLA_DK, GLA_DK), 0)
           == lax.broadcasted_iota(jnp.int32, (GLA_DK, GLA_DK), 1))
    tile_decay = jnp.exp(cum_end)
    decay_col = [jnp.sum(jnp.where(eye, tile_decay[:, hd * GLA_DK:(hd + 1) * GLA_DK], 0.0),
                         axis=1, keepdims=True) for hd in range(GLA_HEADS)]
    ext = jnp.concatenate([carry_ref[s], m], axis=0)
    carry_ref[s] = m[ts - SUBLANE:, :]
    m1 = pltpu.roll(ext, 1, 0)[SUBLANE:, :]
    m2 = pltpu.roll(ext, 2, 0)[SUBLANE:, :]
    cw = conv_w_ref[...]
    u = cb * (cw[0:1, :] * m2 + cw[1:2, :] * m1 + cw[2:3, :] * m)
    yield

    attn, s_old = [], []
    for hd in range(GLA_HEADS):
        ks = slice(hd * GLA_DK, (hd + 1) * GLA_DK)
        vs = slice(hd * GLA_DV, (hd + 1) * GLA_DV)
        s_t = state_ref[s, hd]
        s_old.append(_bf16(s_t))
        blocks = []
        for c in range(n_chunks):
            rs = slice(c * GLA_CHUNK, (c + 1) * GLA_CHUNK)
            a_c = lax.dot_general(q_in[rs, ks], k_rel[c][:, ks], _NT,
                                  preferred_element_type=F32)
            if a_c.shape[1] < ts:
                a_c = jnp.concatenate(
                    [a_c, jnp.zeros((GLA_CHUNK, ts - a_c.shape[1]), F32)], axis=1)
            blocks.append(a_c)
        attn.append(_bf16(jnp.where(causal, jnp.concatenate(blocks, axis=0), 0.0)))
        state_ref[s, hd] = s_t * decay_col[hd] + lax.dot_general(
            k_end[:, ks], v[:, vs], _TN, preferred_element_type=F32)
    z_c = proj_b(OFF_ZC, CONV_WIDTH)
    gated_c = _bf16(u * (z_c * _sigmoid(z_c)))
    yield
    g_gla = g_gla_ref[...]
    gated = []
    for hd in range(GLA_HEADS):
        ks = slice(hd * GLA_DK, (hd + 1) * GLA_DK)
        vs = slice(hd * GLA_DV, (hd + 1) * GLA_DV)
        o = (jnp.dot(q_tile[:, ks], s_old[hd], preferred_element_type=F32)
             + jnp.dot(attn[hd], v[:, vs], preferred_element_type=F32))
        gated.append(_bf16(_rms(o, g_gla) * act_a[:, vs]))
    b_merge = b_merge_ref[...]
    gate_c = _bf16(_sigmoid(proj_b(OFF_GC, D_MODEL) + b_merge[:, D_MODEL:]))
    yield
    y_c = _bf16(gate_c * jnp.dot(gated_c, w_c_ref[...], preferred_element_type=F32))
    gate_a = _bf16(_sigmoid(proj_b(OFF_GA, D_MODEL) + b_merge[:, :D_MODEL]))
    yield
    y_a = jnp.dot(jnp.concatenate(gated, axis=1), w_a_ref[...], preferred_element_type=F32)
    merged = _bf16(gate_a * y_a + y_c)
    yield

    x1 = x + jnp.dot(merged, w_out_ref[...], preferred_element_type=F32)
    yield
    hn = _bf16(_rms(x1, g_ple_ref[...]))
    yield
    ple_gate = _sigmoid(jnp.dot(hn, w_pg_ref[...], preferred_element_type=F32))
    x2 = x1 + ple_gate * jnp.dot(_bf16(p_ref[s]), w_pe_ref[...], preferred_element_type=F32)
    yield
    if apply_final_norm:
        x2 = _rms(x2, g_fin_ref[...])
    out_ref[s] = x2


def _layer_kernel(x_ref, p_ref, w_in_t_hbm, wa_hbm, wc_hbm, wo_hbm, wpg_hbm, wpe_hbm,
                  gate_rows_ref, w_gk2_f32_ref, g_mix_ref, b_gk_ref, g_gla_ref, conv_w_ref,
                  b_merge_ref, g_ple_ref, g_fin_ref, out_ref,
                  w_a_in_ref, w_gkl_ref, w_b_in_ref, w_gk2_ref, w_a_ref, w_c_ref, w_out_ref,
                  w_pg_ref, w_pe_ref, stage_ref, sem_ref, state_ref, carry_ref,
                  *, layer, apply_final_norm):
    @pl.when((pl.program_id(0) == 0) & (pl.program_id(1) == 0))
    def _():
        _load_weights(layer, w_in_t_hbm, wa_hbm, wc_hbm, wo_hbm, wpg_hbm, wpe_hbm, gate_rows_ref,
                      w_gk2_f32_ref, w_a_in_ref, w_gkl_ref, w_b_in_ref, w_gk2_ref, w_a_ref,
                      w_c_ref, w_out_ref, w_pg_ref, w_pe_ref, stage_ref, sem_ref)

    @pl.when(pl.program_id(1) == 0)
    def _():
        state_ref[...] = jnp.zeros_like(state_ref)
        carry_ref[...] = jnp.zeros_like(carry_ref)

    live = [_tile_stages(s, x_ref, p_ref, w_a_in_ref, w_gkl_ref, w_b_in_ref, w_gk2_ref, w_a_ref,
                         w_c_ref, w_out_ref, w_pg_ref, w_pe_ref, g_mix_ref, b_gk_ref, g_gla_ref,
                         conv_w_ref, b_merge_ref, g_ple_ref, g_fin_ref, out_ref, state_ref,
                         carry_ref, apply_final_norm)
            for s in range(x_ref.shape[0])]
    while live:
        for g in list(live):
            try:
                next(g)
            except StopIteration:
                live.remove(g)


def _const_spec(shape, index=None):
    index = (0,) * len(shape) if index is None else index
    return pl.BlockSpec(shape, lambda b, t: index, pipeline_mode=pl.Buffered(1))


def _layer(layer, x, p_i, w_in, w_gk2, big_weights, vectors, *, apply_final_norm):
    bsz, seq, d = x.shape
    assert d == D_MODEL and seq % SEQ_TILE == 0 and SEQ_TILE % LANE == 0
    assert w_in.shape[1:] == (D_MODEL, IN_COLS) and COLS_A % GATE_RANK == 0
    nb = 2 if bsz % 2 == 0 else 1
    w_in_t = jnp.swapaxes(w_in, 1, 2)
    tile = lambda width: pl.BlockSpec((nb, SEQ_TILE, width), lambda b, t: (b, t, 0))
    in_hbm = pl.BlockSpec(memory_space=pltpu.HBM)
    in_specs = ([tile(D_MODEL), tile(PLE_DIM)] + [in_hbm] * (1 + len(big_weights))
                + [_const_spec((None, GATE_RANK, D_MODEL), (layer, COLS_A // GATE_RANK, 0)),
                   _const_spec((None, GATE_RANK, GLA_KW), (layer, 0, 0))]
                + [_const_spec(vv.shape) if vv.ndim == 2
                   else _const_spec((None,) + vv.shape[1:], (layer, 0, 0)) for vv in vectors])
    bf16 = jnp.bfloat16
    return pl.pallas_call(
        functools.partial(_layer_kernel, layer=layer, apply_final_norm=apply_final_norm),
        grid=(bsz // nb, seq // SEQ_TILE),
        in_specs=in_specs,
        out_specs=tile(D_MODEL),
        out_shape=jax.ShapeDtypeStruct(x.shape, jnp.float32),
        scratch_shapes=[
            pltpu.VMEM((COLS_A // WCHUNK, D_MODEL, WCHUNK), bf16),
            pltpu.VMEM((D_MODEL, LANE), bf16),
            pltpu.VMEM((COLS_B // WCHUNK, D_MODEL, WCHUNK), bf16),
            pltpu.VMEM((LANE, GLA_KW), bf16),
            pltpu.VMEM((GLA_VW, D_MODEL), bf16),
            pltpu.VMEM((CONV_WIDTH, D_MODEL), bf16),
            pltpu.VMEM((D_MODEL, D_MODEL), bf16),
            pltpu.VMEM((D_MODEL, D_MODEL), bf16),
            pltpu.VMEM((PLE_DIM, D_MODEL), bf16),
            pltpu.VMEM((WSLOTS, WCHUNK, D_MODEL), jnp.float32),
            pltpu.SemaphoreType.DMA((WSLOTS,)),
            pltpu.VMEM((nb, GLA_HEADS, GLA_DK, GLA_DV), jnp.float32),
            pltpu.VMEM((nb, SUBLANE, CONV_WIDTH), jnp.float32),
        ],
        compiler_params=pltpu.CompilerParams(
            dimension_semantics=("arbitrary", "arbitrary"),
            vmem_limit_bytes=VMEM_LIMIT_BYTES),
        name="hybrid_layer",
    )(x, p_i, w_in_t, *big_weights, w_in_t, w_gk2, *vectors)


def kernel(x, p, norm_mix_g, w_in, b_merge, w_gk2, b_gk, gla_norm_g, conv_w, w_branch_a,
           w_branch_c, w_out, norm_ple_g, w_ple_gate, w_ple_proj, norm_final_g):
    depth = w_in.shape[0]
    row = lambda a: a.reshape(1, -1).astype(jnp.float32)
    for i in range(depth):
        vectors = [row(norm_mix_g[i]), row(b_gk[i]), row(gla_norm_g[i]),
                   conv_w.astype(jnp.float32), row(b_merge[i]), row(norm_ple_g[i]),
                   row(norm_final_g)]
        x = _layer(i, x, p[i], w_in, w_gk2,
                   [w_branch_a, w_branch_c, w_out, w_ple_gate, w_ple_proj], vectors,
                   apply_final_norm=(i == depth - 1))
    return x
```

```python
import functools

import jax
import jax.numpy as jnp
from jax import lax
from jax.experimental import pallas as pl
from jax.experimental.pallas import tpu as pltpu

D_MODEL = 1024
GLA_HEADS = 4
GLA_DK = 128
GLA_DV = 256
GLA_KW = GLA_HEADS * GLA_DK
GLA_VW = GLA_HEADS * GLA_DV
GATE_RANK = 16
GATE_NORMALIZER = 16.0
GLA_CHUNK = 64
CONV_WIDTH = 1024
CONV_K = 3
PLE_DIM = 256
EPS = 1e-6

LANE = 128
SUBLANE = 8
SEQ_TILE = 256
WCHUNK = 256
WSLOTS = 4
VMEM_LIMIT_BYTES = 58 * 1024 * 1024

OFF_Q = 0
OFF_K = OFF_Q + GLA_KW
OFF_V = OFF_K + GLA_KW
OFF_ZA = OFF_V + GLA_VW
COLS_A = OFF_ZA + GLA_VW
OFF_CB = 0
OFF_CC = OFF_CB + CONV_WIDTH
OFF_XC = OFF_CC + CONV_WIDTH
OFF_ZC = OFF_XC + CONV_WIDTH
OFF_GA = OFF_ZC + CONV_WIDTH
OFF_GC = OFF_GA + D_MODEL
COLS_B = OFF_GC + D_MODEL
IN_COLS = COLS_A + GATE_RANK + COLS_B

_NT = (((1,), (1,)), ((), ()))
_TN = (((0,), (0,)), ((), ()))
F32 = jnp.float32


def _bf16(a):
    return a.astype(jnp.bfloat16)


def _rms(xf, g):
    ms = jnp.mean(xf * xf, axis=-1, keepdims=True)
    return xf * lax.rsqrt(ms + EPS) * g


def _sigmoid(a):
    return 0.5 * jnp.tanh(0.5 * a) + 0.5


def _log_sigmoid(a):
    return jnp.minimum(a, 0.0) - jnp.log(1.0 + jnp.exp(-jnp.abs(a)))


def _round_up(n, m):
    return (n + m - 1) // m * m


def _stream_chunks(n_chunks, src_chunk, consume, stage_ref, sem_ref):
    ahead = WSLOTS - 1

    def copy(c):
        slot = c % WSLOTS
        return pltpu.make_async_copy(src_chunk(c), stage_ref.at[slot], sem_ref.at[slot])

    for c in range(min(ahead, n_chunks)):
        copy(c).start()

    def body(c, carry):
        copy(c).wait()

        @pl.when(c + ahead < n_chunks)
        def _():
            copy(c + ahead).start()

        consume(c, stage_ref[c % WSLOTS])
        return carry

    lax.fori_loop(0, n_chunks, body, 0)


def _load_weights(layer, w_in_t_hbm, wa_hbm, wc_hbm, wo_hbm, wpg_hbm, wpe_hbm, gate_rows_ref,
                  w_gk2_f32_ref, w_a_in_ref, w_gkl_ref, w_b_in_ref, w_gk2_ref, w_a_ref, w_c_ref,
                  w_out_ref, w_pg_ref, w_pe_ref, stage_ref, sem_ref):
    def in_proj_rows(first_row):
        return lambda c: w_in_t_hbm.at[layer, pl.ds(first_row + c * WCHUNK, WCHUNK), :]

    def transposed_into(dst_ref):
        def consume(c, chunk):
            dst_ref[c] = _bf16(chunk.T)
        return consume

    def rows_into(dst_ref):
        def consume(c, chunk):
            dst_ref[pl.ds(pl.multiple_of(c * WCHUNK, WCHUNK), WCHUNK), :] = _bf16(chunk)
        return consume

    _stream_chunks(COLS_A // WCHUNK, in_proj_rows(0), transposed_into(w_a_in_ref),
                   stage_ref, sem_ref)
    _stream_chunks(COLS_B // WCHUNK, in_proj_rows(COLS_A + GATE_RANK),
                   transposed_into(w_b_in_ref), stage_ref, sem_ref)
    for hbm, dst in ((wa_hbm, w_a_ref), (wc_hbm, w_c_ref), (wo_hbm, w_out_ref),
                     (wpg_hbm, w_pg_ref), (wpe_hbm, w_pe_ref)):
        _stream_chunks(dst.shape[0] // WCHUNK,
                       lambda c, hbm=hbm: hbm.at[layer, pl.ds(c * WCHUNK, WCHUNK), :],
                       rows_into(dst), stage_ref, sem_ref)
    gate_rows = jnp.concatenate(
        [gate_rows_ref[...], jnp.zeros((LANE - GATE_RANK, D_MODEL), F32)], axis=0)
    w_gkl_ref[...] = _bf16(gate_rows.T)
    w_gk2_ref[...] = jnp.zeros_like(w_gk2_ref)
    w_gk2_ref[:GATE_RANK, :] = _bf16(w_gk2_f32_ref[...])


def _tile_stages(s, x_ref, p_ref, w_a_in_ref, w_gkl_ref, w_b_in_ref, w_gk2_ref, w_a_ref,
                 w_c_ref, w_out_ref, w_pg_ref, w_pe_ref, g_mix_ref, b_gk_ref, g_gla_ref,
                 conv_w_ref, b_merge_ref, g_ple_ref, g_fin_ref, out_ref, state_ref, carry_ref,
                 apply_final_norm):
    ts = x_ref.shape[1]
    n_chunks = ts // GLA_CHUNK

    x = x_ref[s]
    h = _bf16(_rms(x, g_mix_ref[...]))
    yield

    def proj(w_ref, off, width):
        return jnp.concatenate(
            [jnp.dot(h, w_ref[c], preferred_element_type=F32)
             for c in range(off // WCHUNK, (off + width) // WCHUNK)], axis=1)

    proj_a = functools.partial(proj, w_a_in_ref)
    proj_b = functools.partial(proj, w_b_in_ref)

    gk_low = _bf16(jnp.dot(h, w_gkl_ref[...], preferred_element_type=F32))
    q = proj_a(OFF_Q, GLA_KW)
    k = proj_a(OFF_K, GLA_KW)
    yield
    pre = jnp.dot(gk_low, w_gk2_ref[...], preferred_element_type=F32) + b_gk_ref[...]
    v = _bf16(proj_a(OFF_V, GLA_VW))
    yield
    z_a = proj_a(OFF_ZA, GLA_VW)
    gk = _log_sigmoid(pre) * (1.0 / GATE_NORMALIZER)
    z_a = _bf16(z_a)
    act_a = z_a * _sigmoid(z_a)
    yield
    m = proj_b(OFF_CC, CONV_WIDTH) * proj_b(OFF_XC, CONV_WIDTH)
    cum = gk
    shift = 1
    while shift < ts:
        cum = cum + jnp.concatenate(
            [jnp.zeros((shift, GLA_KW), F32), cum[:ts - shift, :]], axis=0)
        shift *= 2
    row = lax.broadcasted_iota(jnp.int32, (ts, ts), 0)
    col = lax.broadcasted_iota(jnp.int32, (ts, ts), 1)
    causal = col <= row
    yield
    cb = proj_b(OFF_CB, CONV_WIDTH)
    cum_end = cum[ts - 1:ts, :]
    starts = [jnp.zeros((1, GLA_KW), F32)]
    starts += [cum[c * GLA_CHUNK - 1:c * GLA_CHUNK, :] for c in range(1, n_chunks)]
    start_full = jnp.concatenate(
        [jnp.broadcast_to(st, (GLA_CHUNK, GLA_KW)) for st in starts], axis=0)
    exp_start_full = jnp.concatenate(
        [jnp.broadcast_to(jnp.exp(st), (GLA_CHUNK, GLA_KW)) for st in starts], axis=0)
    q_loc = q * (jnp.exp(cum - start_full) * (GLA_DK ** -0.5))
    q_in = _bf16(q_loc)
    q_tile = _bf16(q_loc * exp_start_full)
    k_end = _bf16(k * jnp.exp(cum_end - cum))
    k_rel = []
    for c in range(n_chunks):
        n = _round_up((c + 1) * GLA_CHUNK, LANE)
        k_rel.append(_bf16(k[:n, :] * jnp.exp(starts[c] - cum[:n, :])))
    tile_decay = jnp.exp(cum_end)
    ext = jnp.concatenate([carry_ref[s], m], axis=0)
    carry_ref[s] = m[ts - SUBLANE:, :]
    m1 = pltpu.roll(ext, 1, 0)[SUBLANE:, :]
    m2 = pltpu.roll(ext, 2, 0)[SUBLANE:, :]
    cw = conv_w_ref[...]
    u = cb * (cw[0:1, :] * m2 + cw[1:2, :] * m1 + cw[2:3, :] * m)
    yield

    attn, s_old = [], []
    for hd in range(GLA_HEADS):
        ks = slice(hd * GLA_DK, (hd + 1) * GLA_DK)
        vs = slice(hd * GLA_DV, (hd + 1) * GLA_DV)
        s_t = state_ref[s, hd]
        s_old.append(_bf16(s_t))
        blocks = []
        for c in range(n_chunks):
            rs = slice(c * GLA_CHUNK, (c + 1) * GLA_CHUNK)
            a_c = lax.dot_general(q_in[rs, ks], k_rel[c][:, ks], _NT,
                                  preferred_element_type=F32)
            if a_c.shape[1] < ts:
                a_c = jnp.concatenate(
                    [a_c, jnp.zeros((GLA_CHUNK, ts - a_c.shape[1]), F32)], axis=1)
            blocks.append(a_c)
        attn.append(_bf16(jnp.where(causal, jnp.concatenate(blocks, axis=0), 0.0)))
        state_ref[s, hd] = s_t * tile_decay[:, ks] + lax.dot_general(
            v[:, vs], k_end[:, ks], _TN, preferred_element_type=F32)
    z_c = proj_b(OFF_ZC, CONV_WIDTH)
    z_c = _bf16(z_c)
    gated_c = _bf16(u) * (z_c * _sigmoid(z_c))
    yield
    g_gla = g_gla_ref[...]
    gated = []
    for hd in range(GLA_HEADS):
        ks = slice(hd * GLA_DK, (hd + 1) * GLA_DK)
        vs = slice(hd * GLA_DV, (hd + 1) * GLA_DV)
        o = (lax.dot_general(q_tile[:, ks], s_old[hd], _NT, preferred_element_type=F32)
             + jnp.dot(attn[hd], v[:, vs], preferred_element_type=F32))
        gated.append(_bf16(_rms(o, g_gla) * act_a[:, vs]))
    b_merge = b_merge_ref[...]
    gate_c = _sigmoid(_bf16(proj_b(OFF_GC, D_MODEL) + b_merge[:, D_MODEL:]))
    yield
    y_c = _bf16(gate_c * jnp.dot(gated_c, w_c_ref[...], preferred_element_type=F32))
    gate_a = _sigmoid(_bf16(proj_b(OFF_GA, D_MODEL) + b_merge[:, :D_MODEL]))
    yield
    y_a = jnp.dot(jnp.concatenate(gated, axis=1), w_a_ref[...], preferred_element_type=F32)
    merged = _bf16(gate_a * y_a + y_c)
    yield

    x1 = x + jnp.dot(merged, w_out_ref[...], preferred_element_type=F32)
    yield
    hn = _bf16(_rms(x1, g_ple_ref[...]))
    yield
    ple_gate = _sigmoid(jnp.dot(hn, w_pg_ref[...], preferred_element_type=F32))
    x2 = x1 + ple_gate * jnp.dot(_bf16(p_ref[s]), w_pe_ref[...], preferred_element_type=F32)
    yield
    if apply_final_norm:
        x2 = _rms(x2, g_fin_ref[...])
    out_ref[s] = x2


def _layer_kernel(x_ref, p_ref, w_in_t_hbm, wa_hbm, wc_hbm, wo_hbm, wpg_hbm, wpe_hbm,
                  gate_rows_ref, w_gk2_f32_ref, g_mix_ref, b_gk_ref, g_gla_ref, conv_w_ref,
                  b_merge_ref, g_ple_ref, g_fin_ref, out_ref,
                  w_a_in_ref, w_gkl_ref, w_b_in_ref, w_gk2_ref, w_a_ref, w_c_ref, w_out_ref,
                  w_pg_ref, w_pe_ref, stage_ref, sem_ref, state_ref, carry_ref,
                  *, layer, apply_final_norm):
    @pl.when((pl.program_id(0) == 0) & (pl.program_id(1) == 0))
    def _():
        _load_weights(layer, w_in_t_hbm, wa_hbm, wc_hbm, wo_hbm, wpg_hbm, wpe_hbm, gate_rows_ref,
                      w_gk2_f32_ref, w_a_in_ref, w_gkl_ref, w_b_in_ref, w_gk2_ref, w_a_ref,
                      w_c_ref, w_out_ref, w_pg_ref, w_pe_ref, stage_ref, sem_ref)

    @pl.when(pl.program_id(1) == 0)
    def _():
        state_ref[...] = jnp.zeros_like(state_ref)
        carry_ref[...] = jnp.zeros_like(carry_ref)

    live = [_tile_stages(s, x_ref, p_ref, w_a_in_ref, w_gkl_ref, w_b_in_ref, w_gk2_ref, w_a_ref,
                         w_c_ref, w_out_ref, w_pg_ref, w_pe_ref, g_mix_ref, b_gk_ref, g_gla_ref,
                         conv_w_ref, b_merge_ref, g_ple_ref, g_fin_ref, out_ref, state_ref,
                         carry_ref, apply_final_norm)
            for s in range(x_ref.shape[0])]
    while live:
        for g in list(live):
            try:
                next(g)
            except StopIteration:
                live.remove(g)


def _const_spec(shape, index=None):
    index = (0,) * len(shape) if index is None else index
    return pl.BlockSpec(shape, lambda b, t: index, pipeline_mode=pl.Buffered(1))


def _layer(layer, x, p_i, w_in, w_gk2, big_weights, vectors, *, apply_final_norm):
    bsz, seq, d = x.shape
    assert d == D_MODEL and seq % SEQ_TILE == 0 and SEQ_TILE % LANE == 0
    assert w_in.shape[1:] == (D_MODEL, IN_COLS) and COLS_A % GATE_RANK == 0
    nb = 2 if bsz % 2 == 0 else 1
    w_in_t = jnp.swapaxes(w_in, 1, 2)
    tile = lambda width: pl.BlockSpec((nb, SEQ_TILE, width), lambda b, t: (b, t, 0))
    in_hbm = pl.BlockSpec(memory_space=pltpu.HBM)
    in_specs = ([tile(D_MODEL), tile(PLE_DIM)] + [in_hbm] * (1 + len(big_weights))
                + [_const_spec((None, GATE_RANK, D_MODEL), (layer, COLS_A // GATE_RANK, 0)),
                   _const_spec((None, GATE_RANK, GLA_KW), (layer, 0, 0))]
                + [_const_spec(vv.shape) if vv.ndim == 2
                   else _const_spec((None,) + vv.shape[1:], (layer, 0, 0)) for vv in vectors])
    bf16 = jnp.bfloat16
    return pl.pallas_call(
        functools.partial(_layer_kernel, layer=layer, apply_final_norm=apply_final_norm),
        grid=(bsz // nb, seq // SEQ_TILE),
        in_specs=in_specs,
        out_specs=tile(D_MODEL),
        out_shape=jax.ShapeDtypeStruct(x.shape, jnp.float32),
        scratch_shapes=[
            pltpu.VMEM((COLS_A // WCHUNK, D_MODEL, WCHUNK), bf16),
            pltpu.VMEM((D_MODEL, LANE), bf16),
            pltpu.VMEM((COLS_B // WCHUNK, D_MODEL, WCHUNK), bf16),
            pltpu.VMEM((LANE, GLA_KW), bf16),
            pltpu.VMEM((GLA_VW, D_MODEL), bf16),
            pltpu.VMEM((CONV_WIDTH, D_MODEL), bf16),
            pltpu.VMEM((D_MODEL, D_MODEL), bf16),
            pltpu.VMEM((D_MODEL, D_MODEL), bf16),
            pltpu.VMEM((PLE_DIM, D_MODEL), bf16),
            pltpu.VMEM((WSLOTS, WCHUNK, D_MODEL), jnp.float32),
            pltpu.SemaphoreType.DMA((WSLOTS,)),
            pltpu.VMEM((nb, GLA_HEADS, GLA_DV, GLA_DK), jnp.float32),
            pltpu.VMEM((nb, SUBLANE, CONV_WIDTH), jnp.float32),
        ],
        compiler_params=pltpu.CompilerParams(
            dimension_semantics=("arbitrary", "arbitrary"),
            vmem_limit_bytes=VMEM_LIMIT_BYTES),
        name="hybrid_layer",
    )(x, p_i, w_in_t, *big_weights, w_in_t, w_gk2, *vectors)


def kernel(x, p, norm_mix_g, w_in, b_merge, w_gk2, b_gk, gla_norm_g, conv_w, w_branch_a,
           w_branch_c, w_out, norm_ple_g, w_ple_gate, w_ple_proj, norm_final_g):
    depth = w_in.shape[0]
    row = lambda a: a.reshape(1, -1).astype(jnp.float32)
    for i in range(depth):
        vectors = [row(norm_mix_g[i]), row(b_gk[i]), row(gla_norm_g[i]),
                   conv_w.astype(jnp.float32), row(b_merge[i]), row(norm_ple_g[i]),
                   row(norm_final_g)]
        x = _layer(i, x, p[i], w_in, w_gk2,
                   [w_branch_a, w_branch_c, w_out, w_ple_gate, w_ple_proj], vectors,
                   apply_final_norm=(i == depth - 1))
    return x
```

```python
import functools

import jax
import jax.numpy as jnp
from jax import lax
from jax.experimental import pallas as pl
from jax.experimental.pallas import tpu as pltpu

D_MODEL = 1024
GLA_HEADS = 4
GLA_DK = 128
GLA_DV = 256
GLA_KW = GLA_HEADS * GLA_DK
GLA_VW = GLA_HEADS * GLA_DV
GATE_RANK = 16
GATE_NORMALIZER = 16.0
GLA_CHUNK = 64
CONV_WIDTH = 1024
CONV_K = 3
PLE_DIM = 256
EPS = 1e-6

LANE = 128
SUBLANE = 8
SEQ_TILE = 256
WCHUNK = 256
WSLOTS = 4
VMEM_LIMIT_BYTES = 58 * 1024 * 1024

OFF_Q = 0
OFF_K = OFF_Q + GLA_KW
OFF_V = OFF_K + GLA_KW
OFF_ZA = OFF_V + GLA_VW
COLS_A = OFF_ZA + GLA_VW
OFF_CB = 0
OFF_CC = OFF_CB + CONV_WIDTH
OFF_XC = OFF_CC + CONV_WIDTH
OFF_ZC = OFF_XC + CONV_WIDTH
OFF_GA = OFF_ZC + CONV_WIDTH
OFF_GC = OFF_GA + D_MODEL
COLS_B = OFF_GC + D_MODEL
IN_COLS = COLS_A + GATE_RANK + COLS_B

_NT = (((1,), (1,)), ((), ()))
_TN = (((0,), (0,)), ((), ()))
F32 = jnp.float32


def _bf16(a):
    return a.astype(jnp.bfloat16)


def _rms(xf, g):
    ms = jnp.mean(xf * xf, axis=-1, keepdims=True)
    return xf * lax.rsqrt(ms + EPS) * g


def _sigmoid(a):
    return 0.5 * jnp.tanh(0.5 * a) + 0.5


def _log_sigmoid(a):
    return jnp.minimum(a, 0.0) - jnp.log(1.0 + jnp.exp(-jnp.abs(a)))


def _round_up(n, m):
    return (n + m - 1) // m * m


def _stream_chunks(n_chunks, src_chunk, consume, stage_ref, sem_ref):
    ahead = WSLOTS - 1

    def copy(c):
        slot = c % WSLOTS
        return pltpu.make_async_copy(src_chunk(c), stage_ref.at[slot], sem_ref.at[slot])

    for c in range(min(ahead, n_chunks)):
        copy(c).start()

    def body(c, carry):
        copy(c).wait()

        @pl.when(c + ahead < n_chunks)
        def _():
            copy(c + ahead).start()

        consume(c, stage_ref[c % WSLOTS])
        return carry

    lax.fori_loop(0, n_chunks, body, 0)


def _load_weights(layer, w_in_t_hbm, wa_hbm, wc_hbm, wo_hbm, wpg_hbm, wpe_hbm, gate_rows_ref,
                  w_gk2_f32_ref, w_a_in_ref, w_gkl_ref, w_b_in_ref, w_gk2_ref, w_a_ref, w_c_ref,
                  w_out_ref, w_pg_ref, w_pe_ref, stage_ref, sem_ref):
    def in_proj_rows(first_row):
        return lambda c: w_in_t_hbm.at[layer, pl.ds(first_row + c * WCHUNK, WCHUNK), :]

    def transposed_into(dst_ref):
        def consume(c, chunk):
            dst_ref[c] = _bf16(chunk.T)
        return consume

    def rows_into(dst_ref):
        def consume(c, chunk):
            dst_ref[pl.ds(pl.multiple_of(c * WCHUNK, WCHUNK), WCHUNK), :] = _bf16(chunk)
        return consume

    _stream_chunks(COLS_A // WCHUNK, in_proj_rows(0), transposed_into(w_a_in_ref),
                   stage_ref, sem_ref)
    _stream_chunks(COLS_B // WCHUNK, in_proj_rows(COLS_A + GATE_RANK),
                   transposed_into(w_b_in_ref), stage_ref, sem_ref)
    for hbm, dst in ((wa_hbm, w_a_ref), (wc_hbm, w_c_ref), (wo_hbm, w_out_ref),
                     (wpg_hbm, w_pg_ref), (wpe_hbm, w_pe_ref)):
        _stream_chunks(dst.shape[0] // WCHUNK,
                       lambda c, hbm=hbm: hbm.at[layer, pl.ds(c * WCHUNK, WCHUNK), :],
                       rows_into(dst), stage_ref, sem_ref)
    gate_rows = jnp.concatenate(
        [gate_rows_ref[...], jnp.zeros((LANE - GATE_RANK, D_MODEL), F32)], axis=0)
    w_gkl_ref[...] = _bf16(gate_rows.T)
    w_gk2_ref[...] = jnp.zeros_like(w_gk2_ref)
    w_gk2_ref[:GATE_RANK, :] = _bf16(w_gk2_f32_ref[...])


def _tile_stages(s, x_ref, p_ref, w_a_in_ref, w_gkl_ref, w_b_in_ref, w_gk2_ref, w_a_ref,
                 w_c_ref, w_out_ref, w_pg_ref, w_pe_ref, g_mix_ref, b_gk_ref, g_gla_ref,
                 conv_w_ref, b_merge_ref, g_ple_ref, g_fin_ref, out_ref, state_ref, carry_ref,
                 apply_final_norm):
    ts = x_ref.shape[1]
    n_chunks = ts // GLA_CHUNK

    x = x_ref[s]
    inv = jnp.broadcast_to(
        lax.rsqrt(jnp.mean(x * x, axis=-1, keepdims=True) + EPS), (ts, D_MODEL))
    h = _bf16(x * g_mix_ref[...])
    yield

    def proj(w_ref, off, width):
        return inv[:, :width] * jnp.concatenate(
            [jnp.dot(h, w_ref[c], preferred_element_type=F32)
             for c in range(off // WCHUNK, (off + width) // WCHUNK)], axis=1)

    proj_a = functools.partial(proj, w_a_in_ref)
    proj_b = functools.partial(proj, w_b_in_ref)

    gk_low = _bf16(inv[:, :LANE] * jnp.dot(h, w_gkl_ref[...], preferred_element_type=F32))
    q = proj_a(OFF_Q, GLA_KW)
    k = proj_a(OFF_K, GLA_KW)
    yield
    pre = jnp.dot(gk_low, w_gk2_ref[...], preferred_element_type=F32) + b_gk_ref[...]
    v = _bf16(proj_a(OFF_V, GLA_VW))
    yield
    z_a = proj_a(OFF_ZA, GLA_VW)
    gk = _log_sigmoid(pre) * (1.0 / GATE_NORMALIZER)
    act_a = _bf16(z_a * _sigmoid(z_a))
    yield
    m = proj_b(OFF_CC, CONV_WIDTH) * proj_b(OFF_XC, CONV_WIDTH)
    cum = gk
    shift = 1
    while shift < ts:
        cum = cum + jnp.concatenate(
            [jnp.zeros((shift, GLA_KW), F32), cum[:ts - shift, :]], axis=0)
        shift *= 2
    row = lax.broadcasted_iota(jnp.int32, (ts, ts), 0)
    col = lax.broadcasted_iota(jnp.int32, (ts, ts), 1)
    causal = col <= row
    yield
    cb = proj_b(OFF_CB, CONV_WIDTH)
    cum_end = cum[ts - 1:ts, :]
    starts = [jnp.zeros((1, GLA_KW), F32)]
    starts += [cum[c * GLA_CHUNK - 1:c * GLA_CHUNK, :] for c in range(1, n_chunks)]
    start_full = jnp.concatenate(
        [jnp.broadcast_to(st, (GLA_CHUNK, GLA_KW)) for st in starts], axis=0)
    exp_start_full = jnp.concatenate(
        [jnp.broadcast_to(jnp.exp(st), (GLA_CHUNK, GLA_KW)) for st in starts], axis=0)
    q_loc = q * (jnp.exp(cum - start_full) * (GLA_DK ** -0.5))
    q_in = _bf16(q_loc)
    q_tile = _bf16(q_loc * exp_start_full)
    k_end = _bf16(k * jnp.exp(cum_end - cum))
    k_rel = []
    for c in range(n_chunks):
        n = _round_up((c + 1) * GLA_CHUNK, LANE)
        k_rel.append(_bf16(k[:n, :] * jnp.exp(starts[c] - cum[:n, :])))
    tile_decay = jnp.exp(cum_end)
    ext = jnp.concatenate([carry_ref[s], m], axis=0)
    carry_ref[s] = m[ts - SUBLANE:, :]
    m1 = pltpu.roll(ext, 1, 0)[SUBLANE:, :]
    m2 = pltpu.roll(ext, 2, 0)[SUBLANE:, :]
    cw = conv_w_ref[...]
    u = cb * (cw[0:1, :] * m2 + cw[1:2, :] * m1 + cw[2:3, :] * m)
    yield

    attn, s_old = [], []
    for hd in range(GLA_HEADS):
        ks = slice(hd * GLA_DK, (hd + 1) * GLA_DK)
        vs = slice(hd * GLA_DV, (hd + 1) * GLA_DV)
        s_t = state_ref[s, hd]
        s_old.append(_bf16(s_t))
        blocks = []
        for c in range(n_chunks):
            rs = slice(c * GLA_CHUNK, (c + 1) * GLA_CHUNK)
            a_c = lax.dot_general(q_in[rs, ks], k_rel[c][:, ks], _NT,
                                  preferred_element_type=F32)
            if a_c.shape[1] < ts:
                a_c = jnp.concatenate(
                    [a_c, jnp.zeros((GLA_CHUNK, ts - a_c.shape[1]), F32)], axis=1)
            blocks.append(a_c)
        attn.append(_bf16(jnp.where(causal, jnp.concatenate(blocks, axis=0), 0.0)))
        state_ref[s, hd] = s_t * tile_decay[:, ks] + lax.dot_general(
            v[:, vs], k_end[:, ks], _TN, preferred_element_type=F32)
    z_c = proj_b(OFF_ZC, CONV_WIDTH)
    gated_c = _bf16(u * (z_c * _sigmoid(z_c)))
    yield
    g_gla = g_gla_ref[...]
    gated = []
    for hd in range(GLA_HEADS):
        ks = slice(hd * GLA_DK, (hd + 1) * GLA_DK)
        vs = slice(hd * GLA_DV, (hd + 1) * GLA_DV)
        o = (lax.dot_general(q_tile[:, ks], s_old[hd], _NT, preferred_element_type=F32)
             + jnp.dot(attn[hd], v[:, vs], preferred_element_type=F32))
        gated.append(_bf16(_rms(o, g_gla) * act_a[:, vs]))
    b_merge = b_merge_ref[...]
    gate_c = _bf16(_sigmoid(proj_b(OFF_GC, D_MODEL) + b_merge[:, D_MODEL:]))
    yield
    y_c = _bf16(gate_c * jnp.dot(gated_c, w_c_ref[...], preferred_element_type=F32))
    gate_a = _bf16(_sigmoid(proj_b(OFF_GA, D_MODEL) + b_merge[:, :D_MODEL]))
    yield
    y_a = jnp.dot(jnp.concatenate(gated, axis=1), w_a_ref[...], preferred_element_type=F32)
    merged = _bf16(gate_a * y_a + y_c)
    yield

    x1 = x + jnp.dot(merged, w_out_ref[...], preferred_element_type=F32)
    yield
    inv1 = lax.rsqrt(jnp.mean(x1 * x1, axis=-1, keepdims=True) + EPS)
    hn = _bf16(x1 * g_ple_ref[...])
    yield
    ple_gate = _sigmoid(inv1 * jnp.dot(hn, w_pg_ref[...], preferred_element_type=F32))
    x2 = x1 + ple_gate * jnp.dot(_bf16(p_ref[s]), w_pe_ref[...], preferred_element_type=F32)
    yield
    if apply_final_norm:
        x2 = _rms(x2, g_fin_ref[...])
    out_ref[s] = x2


def _layer_kernel(x_ref, p_ref, w_in_t_hbm, wa_hbm, wc_hbm, wo_hbm, wpg_hbm, wpe_hbm,
                  gate_rows_ref, w_gk2_f32_ref, g_mix_ref, b_gk_ref, g_gla_ref, conv_w_ref,
                  b_merge_ref, g_ple_ref, g_fin_ref, out_ref,
                  w_a_in_ref, w_gkl_ref, w_b_in_ref, w_gk2_ref, w_a_ref, w_c_ref, w_out_ref,
                  w_pg_ref, w_pe_ref, stage_ref, sem_ref, state_ref, carry_ref,
                  *, layer, apply_final_norm):
    @pl.when((pl.program_id(0) == 0) & (pl.program_id(1) == 0))
    def _():
        _load_weights(layer, w_in_t_hbm, wa_hbm, wc_hbm, wo_hbm, wpg_hbm, wpe_hbm, gate_rows_ref,
                      w_gk2_f32_ref, w_a_in_ref, w_gkl_ref, w_b_in_ref, w_gk2_ref, w_a_ref,
                      w_c_ref, w_out_ref, w_pg_ref, w_pe_ref, stage_ref, sem_ref)

    @pl.when(pl.program_id(1) == 0)
    def _():
        state_ref[...] = jnp.zeros_like(state_ref)
        carry_ref[...] = jnp.zeros_like(carry_ref)

    live = [_tile_stages(s, x_ref, p_ref, w_a_in_ref, w_gkl_ref, w_b_in_ref, w_gk2_ref, w_a_ref,
                         w_c_ref, w_out_ref, w_pg_ref, w_pe_ref, g_mix_ref, b_gk_ref, g_gla_ref,
                         conv_w_ref, b_merge_ref, g_ple_ref, g_fin_ref, out_ref, state_ref,
                         carry_ref, apply_final_norm)
            for s in range(x_ref.shape[0])]
    while live:
        for g in list(live):
            try:
                next(g)
            except StopIteration:
                live.remove(g)


def _const_spec(shape, index=None):
    index = (0,) * len(shape) if index is None else index
    return pl.BlockSpec(shape, lambda b, t: index, pipeline_mode=pl.Buffered(1))


def _layer(layer, x, p_i, w_in, w_gk2, big_weights, vectors, *, apply_final_norm):
    bsz, seq, d = x.shape
    assert d == D_MODEL and seq % SEQ_TILE == 0 and SEQ_TILE % LANE == 0
    assert w_in.shape[1:] == (D_MODEL, IN_COLS) and COLS_A % GATE_RANK == 0
    nb = 2 if bsz % 2 == 0 else 1
    w_in_t = jnp.swapaxes(w_in, 1, 2)
    tile = lambda width: pl.BlockSpec((nb, SEQ_TILE, width), lambda b, t: (b, t, 0))
    in_hbm = pl.BlockSpec(memory_space=pltpu.HBM)
    in_specs = ([tile(D_MODEL), tile(PLE_DIM)] + [in_hbm] * (1 + len(big_weights))
                + [_const_spec((None, GATE_RANK, D_MODEL), (layer, COLS_A // GATE_RANK, 0)),
                   _const_spec((None, GATE_RANK, GLA_KW), (layer, 0, 0))]
                + [_const_spec(vv.shape) if vv.ndim == 2
                   else _const_spec((None,) + vv.shape[1:], (layer, 0, 0)) for vv in vectors])
    bf16 = jnp.bfloat16
    return pl.pallas_call(
        functools.partial(_layer_kernel, layer=layer, apply_final_norm=apply_final_norm),
        grid=(bsz // nb, seq // SEQ_TILE),
        in_specs=in_specs,
        out_specs=tile(D_MODEL),
        out_shape=jax.ShapeDtypeStruct(x.shape, jnp.float32),
        scratch_shapes=[
            pltpu.VMEM((COLS_A // WCHUNK, D_MODEL, WCHUNK), bf16),
            pltpu.VMEM((D_MODEL, LANE), bf16),
            pltpu.VMEM((COLS_B // WCHUNK, D_MODEL, WCHUNK), bf16),
            pltpu.VMEM((LANE, GLA_KW), bf16),
            pltpu.VMEM((GLA_VW, D_MODEL), bf16),
            pltpu.VMEM((CONV_WIDTH, D_MODEL), bf16),
            pltpu.VMEM((D_MODEL, D_MODEL), bf16),
            pltpu.VMEM((D_MODEL, D_MODEL), bf16),
            pltpu.VMEM((PLE_DIM, D_MODEL), bf16),
            pltpu.VMEM((WSLOTS, WCHUNK, D_MODEL), jnp.float32),
            pltpu.SemaphoreType.DMA((WSLOTS,)),
            pltpu.VMEM((nb, GLA_HEADS, GLA_DV, GLA_DK), jnp.float32),
            pltpu.VMEM((nb, SUBLANE, CONV_WIDTH), jnp.float32),
        ],
        compiler_params=pltpu.CompilerParams(
            dimension_semantics=("arbitrary", "arbitrary"),
            vmem_limit_bytes=VMEM_LIMIT_BYTES),
        name="hybrid_layer",
    )(x, p_i, w_in_t, *big_weights, w_in_t, w_gk2, *vectors)


def kernel(x, p, norm_mix_g, w_in, b_merge, w_gk2, b_gk, gla_norm_g, conv_w, w_branch_a,
           w_branch_c, w_out, norm_ple_g, w_ple_gate, w_ple_proj, norm_final_g):
    depth = w_in.shape[0]
    row = lambda a: a.reshape(1, -1).astype(jnp.float32)
    for i in range(depth):
        vectors = [row(norm_mix_g[i]), row(b_gk[i]), row(gla_norm_g[i]),
                   conv_w.astype(jnp.float32), row(b_merge[i]), row(norm_ple_g[i]),
                   row(norm_final_g)]
        x = _layer(i, x, p[i], w_in, w_gk2,
                   [w_branch_a, w_branch_c, w_out, w_ple_gate, w_ple_proj], vectors,
                   apply_final_norm=(i == depth - 1))
    return x
```

```python
import functools

import jax
import jax.numpy as jnp
from jax import lax
from jax.experimental import pallas as pl
from jax.experimental.pallas import tpu as pltpu

D_MODEL = 1024
GLA_HEADS = 4
GLA_DK = 128
GLA_DV = 256
GLA_KW = GLA_HEADS * GLA_DK
GLA_VW = GLA_HEADS * GLA_DV
GATE_RANK = 16
GATE_NORMALIZER = 16.0
GLA_CHUNK = 64
CONV_WIDTH = 1024
CONV_K = 3
PLE_DIM = 256
EPS = 1e-6

LANE = 128
SUBLANE = 8
SEQ_TILE = 256
WCHUNK = 256
WSLOTS = 4
VMEM_LIMIT_BYTES = 58 * 1024 * 1024

OFF_Q = 0
OFF_K = OFF_Q + GLA_KW
OFF_V = OFF_K + GLA_KW
OFF_ZA = OFF_V + GLA_VW
COLS_A = OFF_ZA + GLA_VW
OFF_CB = 0
OFF_CC = OFF_CB + CONV_WIDTH
OFF_XC = OFF_CC + CONV_WIDTH
OFF_ZC = OFF_XC + CONV_WIDTH
OFF_GA = OFF_ZC + CONV_WIDTH
OFF_GC = OFF_GA + D_MODEL
COLS_B = OFF_GC + D_MODEL
IN_COLS = COLS_A + GATE_RANK + COLS_B

_NT = (((1,), (1,)), ((), ()))
_TN = (((0,), (0,)), ((), ()))
F32 = jnp.float32


def _bf16(a):
    return a.astype(jnp.bfloat16)


def _rms(xf, g):
    ms = jnp.mean(xf * xf, axis=-1, keepdims=True)
    return xf * lax.rsqrt(ms + EPS) * g


def _sigmoid(a):
    return 0.5 * jnp.tanh(0.5 * a) + 0.5


def _log_sigmoid(a):
    return jnp.minimum(a, 0.0) - jnp.log(1.0 + jnp.exp(-jnp.abs(a)))


def _round_up(n, m):
    return (n + m - 1) // m * m


def _load_weights(layer, w_in_t_hbm, dense_hbm, gate_rows_ref, w_gk2_f32_ref,
                  w_in_ref, w_gkl_ref, w_gk2_ref, dense_refs, stage_ref, sem_ref):
    ahead = WSLOTS - 1
    n_in = (COLS_A + COLS_B) // WCHUNK
    dense = [(hbm, dst, r) for hbm, dst in zip(dense_hbm, dense_refs)
             for r in range(0, dst.shape[0], WCHUNK)]
    assert ahead <= len(dense) and ahead < n_in

    def copy(src, g):
        return pltpu.make_async_copy(src, stage_ref.at[g % WSLOTS], sem_ref.at[g % WSLOTS])

    def in_proj_copy(c):
        first = c * WCHUNK + jnp.where(c >= COLS_A // WCHUNK, GATE_RANK, 0)
        return copy(w_in_t_hbm.at[layer, pl.ds(first, WCHUNK), :], c)

    def dense_copy(i):
        hbm, _, r = dense[i]
        return copy(hbm.at[layer, pl.ds(r, WCHUNK), :], n_in + i)

    def in_proj_consume(c):
        w_in_ref[c] = _bf16(stage_ref[c % WSLOTS].T)

    for c in range(ahead):
        in_proj_copy(c).start()

    def body(c, carry):
        in_proj_copy(c).wait()
        in_proj_copy(c + ahead).start()
        in_proj_consume(c)
        return carry

    lax.fori_loop(0, n_in - ahead, body, 0)
    for c in range(n_in - ahead, n_in):
        in_proj_copy(c).wait()
        dense_copy(c + ahead - n_in).start()
        in_proj_consume(c)
    for i, (_, dst, r) in enumerate(dense):
        dense_copy(i).wait()
        if i + ahead < len(dense):
            dense_copy(i + ahead).start()
        dst[r:r + WCHUNK, :] = _bf16(stage_ref[(n_in + i) % WSLOTS])

    gate_rows = jnp.concatenate(
        [gate_rows_ref[...], jnp.zeros((LANE - GATE_RANK, D_MODEL), F32)], axis=0)
    w_gkl_ref[...] = _bf16(gate_rows.T)
    w_gk2_ref[...] = jnp.zeros_like(w_gk2_ref)
    w_gk2_ref[:GATE_RANK, :] = _bf16(w_gk2_f32_ref[...])


def _tile_stages(s, x_ref, p_ref, w_in_ref, w_gkl_ref, w_gk2_ref, w_a_ref,
                 w_c_ref, w_out_ref, w_pg_ref, w_pe_ref, g_mix_ref, b_gk_ref, g_gla_ref,
                 conv_w_ref, b_merge_ref, g_ple_ref, g_fin_ref, out_ref, state_ref, carry_ref,
                 apply_final_norm):
    ts = x_ref.shape[1]
    n_chunks = ts // GLA_CHUNK

    x = x_ref[s]
    inv = jnp.broadcast_to(
        lax.rsqrt(jnp.mean(x * x, axis=-1, keepdims=True) + EPS), (ts, D_MODEL))
    h = _bf16(x * g_mix_ref[...])
    yield

    def proj(first_chunk, off, width):
        return inv[:, :width] * jnp.concatenate(
            [jnp.dot(h, w_in_ref[first_chunk + c], preferred_element_type=F32)
             for c in range(off // WCHUNK, (off + width) // WCHUNK)], axis=1)

    proj_a = functools.partial(proj, 0)
    proj_b = functools.partial(proj, COLS_A // WCHUNK)

    gk_low = _bf16(inv[:, :LANE] * jnp.dot(h, w_gkl_ref[...], preferred_element_type=F32))
    q = proj_a(OFF_Q, GLA_KW)
    k = proj_a(OFF_K, GLA_KW)
    yield
    pre = jnp.dot(gk_low, w_gk2_ref[...], preferred_element_type=F32) + b_gk_ref[...]
    v = _bf16(proj_a(OFF_V, GLA_VW))
    yield
    z_a = proj_a(OFF_ZA, GLA_VW)
    gk = _log_sigmoid(pre) * (1.0 / GATE_NORMALIZER)
    act_a = _bf16(z_a * _sigmoid(z_a))
    yield
    m = proj_b(OFF_CC, CONV_WIDTH) * proj_b(OFF_XC, CONV_WIDTH)
    cum = gk
    shift = 1
    while shift < ts:
        cum = cum + jnp.concatenate(
            [jnp.zeros((shift, GLA_KW), F32), cum[:ts - shift, :]], axis=0)
        shift *= 2
    row = lax.broadcasted_iota(jnp.int32, (ts, ts), 0)
    col = lax.broadcasted_iota(jnp.int32, (ts, ts), 1)
    causal = col <= row
    yield
    cb = proj_b(OFF_CB, CONV_WIDTH)
    cum_end = cum[ts - 1:ts, :]
    starts = [jnp.zeros((1, GLA_KW), F32)]
    starts += [cum[c * GLA_CHUNK - 1:c * GLA_CHUNK, :] for c in range(1, n_chunks)]
    start_full = jnp.concatenate(
        [jnp.broadcast_to(st, (GLA_CHUNK, GLA_KW)) for st in starts], axis=0)
    exp_start_full = jnp.concatenate(
        [jnp.broadcast_to(jnp.exp(st), (GLA_CHUNK, GLA_KW)) for st in starts], axis=0)
    q_loc = q * (jnp.exp(cum - start_full) * (GLA_DK ** -0.5))
    q_in = _bf16(q_loc)
    q_tile = _bf16(q_loc * exp_start_full)
    k_end = _bf16(k * jnp.exp(cum_end - cum))
    k_rel = []
    for c in range(n_chunks):
        n = _round_up((c + 1) * GLA_CHUNK, LANE)
        k_rel.append(_bf16(k[:n, :] * jnp.exp(starts[c] - cum[:n, :])))
    tile_decay = jnp.exp(cum_end)
    ext = jnp.concatenate([carry_ref[s], m], axis=0)
    carry_ref[s] = m[ts - SUBLANE:, :]
    m1 = pltpu.roll(ext, 1, 0)[SUBLANE:, :]
    m2 = pltpu.roll(ext, 2, 0)[SUBLANE:, :]
    cw = conv_w_ref[...]
    u = cb * (cw[0:1, :] * m2 + cw[1:2, :] * m1 + cw[2:3, :] * m)
    yield

    attn, s_old = [], []
    for hd in range(GLA_HEADS):
        ks = slice(hd * GLA_DK, (hd + 1) * GLA_DK)
        vs = slice(hd * GLA_DV, (hd + 1) * GLA_DV)
        s_t = state_ref[s, hd]
        s_old.append(_bf16(s_t))
        blocks = []
        for c in range(n_chunks):
            rs = slice(c * GLA_CHUNK, (c + 1) * GLA_CHUNK)
            a_c = lax.dot_general(q_in[rs, ks], k_rel[c][:, ks], _NT,
                                  preferred_element_type=F32)
            if a_c.shape[1] < ts:
                a_c = jnp.concatenate(
                    [a_c, jnp.zeros((GLA_CHUNK, ts - a_c.shape[1]), F32)], axis=1)
            blocks.append(a_c)
        attn.append(_bf16(jnp.where(causal, jnp.concatenate(blocks, axis=0), 0.0)))
        state_ref[s, hd] = s_t * tile_decay[:, ks] + lax.dot_general(
            v[:, vs], k_end[:, ks], _TN, preferred_element_type=F32)
    z_c = proj_b(OFF_ZC, CONV_WIDTH)
    gated_c = _bf16(u * (z_c * _sigmoid(z_c)))
    yield
    g_gla = g_gla_ref[...]
    gated = []
    for hd in range(GLA_HEADS):
        ks = slice(hd * GLA_DK, (hd + 1) * GLA_DK)
        vs = slice(hd * GLA_DV, (hd + 1) * GLA_DV)
        o = (lax.dot_general(q_tile[:, ks], s_old[hd], _NT, preferred_element_type=F32)
             + jnp.dot(attn[hd], v[:, vs], preferred_element_type=F32))
        gated.append(_bf16(_rms(o, g_gla) * act_a[:, vs]))
    b_merge = b_merge_ref[...]
    gate_c = _bf16(_sigmoid(proj_b(OFF_GC, D_MODEL) + b_merge[:, D_MODEL:]))
    yield
    y_c = _bf16(gate_c * jnp.dot(gated_c, w_c_ref[...], preferred_element_type=F32))
    gate_a = _bf16(_sigmoid(proj_b(OFF_GA, D_MODEL) + b_merge[:, :D_MODEL]))
    yield
    y_a = jnp.dot(jnp.concatenate(gated, axis=1), w_a_ref[...], preferred_element_type=F32)
    merged = _bf16(gate_a * y_a + y_c)
    yield

    x1 = x + jnp.dot(merged, w_out_ref[...], preferred_element_type=F32)
    yield
    inv1 = lax.rsqrt(jnp.mean(x1 * x1, axis=-1, keepdims=True) + EPS)
    hn = _bf16(x1 * g_ple_ref[...])
    yield
    ple_gate = _sigmoid(inv1 * jnp.dot(hn, w_pg_ref[...], preferred_element_type=F32))
    x2 = x1 + ple_gate * jnp.dot(_bf16(p_ref[s]), w_pe_ref[...], preferred_element_type=F32)
    yield
    if apply_final_norm:
        x2 = _rms(x2, g_fin_ref[...])
    out_ref[s] = x2


def _layer_kernel(x_ref, p_ref, w_in_t_hbm, wa_hbm, wc_hbm, wo_hbm, wpg_hbm, wpe_hbm,
                  gate_rows_ref, w_gk2_f32_ref, g_mix_ref, b_gk_ref, g_gla_ref, conv_w_ref,
                  b_merge_ref, g_ple_ref, g_fin_ref, out_ref,
                  w_in_ref, w_gkl_ref, w_gk2_ref, w_a_ref, w_c_ref, w_out_ref,
                  w_pg_ref, w_pe_ref, stage_ref, sem_ref, state_ref, carry_ref,
                  *, layer, apply_final_norm):
    @pl.when((pl.program_id(0) == 0) & (pl.program_id(1) == 0))
    def _():
        _load_weights(layer, w_in_t_hbm, (wa_hbm, wc_hbm, wo_hbm, wpg_hbm, wpe_hbm),
                      gate_rows_ref, w_gk2_f32_ref, w_in_ref, w_gkl_ref, w_gk2_ref,
                      (w_a_ref, w_c_ref, w_out_ref, w_pg_ref, w_pe_ref), stage_ref, sem_ref)

    @pl.when(pl.program_id(1) == 0)
    def _():
        state_ref[...] = jnp.zeros_like(state_ref)
        carry_ref[...] = jnp.zeros_like(carry_ref)

    live = [_tile_stages(s, x_ref, p_ref, w_in_ref, w_gkl_ref, w_gk2_ref, w_a_ref,
                         w_c_ref, w_out_ref, w_pg_ref, w_pe_ref, g_mix_ref, b_gk_ref, g_gla_ref,
                         conv_w_ref, b_merge_ref, g_ple_ref, g_fin_ref, out_ref, state_ref,
                         carry_ref, apply_final_norm)
            for s in range(x_ref.shape[0])]
    while live:
        for g in list(live):
            try:
                next(g)
            except StopIteration:
                live.remove(g)


def _const_spec(shape, index=None):
    index = (0,) * len(shape) if index is None else index
    return pl.BlockSpec(shape, lambda b, t: index, pipeline_mode=pl.Buffered(1))


def _layer(layer, x, p_i, w_in, w_gk2, big_weights, vectors, *, apply_final_norm):
    bsz, seq, d = x.shape
    assert d == D_MODEL and seq % SEQ_TILE == 0 and SEQ_TILE % LANE == 0
    assert w_in.shape[1:] == (D_MODEL, IN_COLS) and COLS_A % GATE_RANK == 0
    nb = 2 if bsz % 2 == 0 else 1
    w_in_t = jnp.swapaxes(w_in, 1, 2)
    tile = lambda width: pl.BlockSpec((nb, SEQ_TILE, width), lambda b, t: (b, t, 0))
    in_hbm = pl.BlockSpec(memory_space=pltpu.HBM)
    in_specs = ([tile(D_MODEL), tile(PLE_DIM)] + [in_hbm] * (1 + len(big_weights))
                + [_const_spec((None, GATE_RANK, D_MODEL), (layer, COLS_A // GATE_RANK, 0)),
                   _const_spec((None, GATE_RANK, GLA_KW), (layer, 0, 0))]
                + [_const_spec(vv.shape) if vv.ndim == 2
                   else _const_spec((None,) + vv.shape[1:], (layer, 0, 0)) for vv in vectors])
    bf16 = jnp.bfloat16
    return pl.pallas_call(
        functools.partial(_layer_kernel, layer=layer, apply_final_norm=apply_final_norm),
        grid=(bsz // nb, seq // SEQ_TILE),
        in_specs=in_specs,
        out_specs=tile(D_MODEL),
        out_shape=jax.ShapeDtypeStruct(x.shape, jnp.float32),
        scratch_shapes=[
            pltpu.VMEM(((COLS_A + COLS_B) // WCHUNK, D_MODEL, WCHUNK), bf16),
            pltpu.VMEM((D_MODEL, LANE), bf16),
            pltpu.VMEM((LANE, GLA_KW), bf16),
            pltpu.VMEM((GLA_VW, D_MODEL), bf16),
            pltpu.VMEM((CONV_WIDTH, D_MODEL), bf16),
            pltpu.VMEM((D_MODEL, D_MODEL), bf16),
            pltpu.VMEM((D_MODEL, D_MODEL), bf16),
            pltpu.VMEM((PLE_DIM, D_MODEL), bf16),
            pltpu.VMEM((WSLOTS, WCHUNK, D_MODEL), jnp.float32),
            pltpu.SemaphoreType.DMA((WSLOTS,)),
            pltpu.VMEM((nb, GLA_HEADS, GLA_DV, GLA_DK), jnp.float32),
            pltpu.VMEM((nb, SUBLANE, CONV_WIDTH), jnp.float32),
        ],
        compiler_params=pltpu.CompilerParams(
            dimension_semantics=("arbitrary", "arbitrary"),
            vmem_limit_bytes=VMEM_LIMIT_BYTES),
        name="hybrid_layer",
    )(x, p_i, w_in_t, *big_weights, w_in_t, w_gk2, *vectors)


def kernel(x, p, norm_mix_g, w_in, b_merge, w_gk2, b_gk, gla_norm_g, conv_w, w_branch_a,
           w_branch_c, w_out, norm_ple_g, w_ple_gate, w_ple_proj, norm_final_g):
    depth = w_in.shape[0]
    row = lambda a: a.reshape(1, -1).astype(jnp.float32)
    for i in range(depth):
        vectors = [row(norm_mix_g[i]), row(b_gk[i]), row(gla_norm_g[i]),
                   conv_w.astype(jnp.float32), row(b_merge[i]), row(norm_ple_g[i]),
                   row(norm_final_g)]
        x = _layer(i, x, p[i], w_in, w_gk2,
                   [w_branch_a, w_branch_c, w_out, w_ple_gate, w_ple_proj], vectors,
                   apply_final_norm=(i == depth - 1))
    return x
```

```python
import functools

import jax
import jax.numpy as jnp
from jax import lax
from jax.experimental import pallas as pl
from jax.experimental.pallas import tpu as pltpu

D_MODEL = 1024
GLA_HEADS = 4
GLA_DK = 128
GLA_DV = 256
GLA_KW = GLA_HEADS * GLA_DK
GLA_VW = GLA_HEADS * GLA_DV
GATE_RANK = 16
GATE_NORMALIZER = 16.0
GLA_CHUNK = 64
CONV_WIDTH = 1024
CONV_K = 3
PLE_DIM = 256
EPS = 1e-6

LANE = 128
SUBLANE = 8
SEQ_TILE = 256
WCHUNK = 256
WSLOTS = 4
VMEM_LIMIT_BYTES = 58 * 1024 * 1024

OFF_Q = 0
OFF_K = OFF_Q + GLA_KW
OFF_V = OFF_K + GLA_KW
OFF_ZA = OFF_V + GLA_VW
COLS_A = OFF_ZA + GLA_VW
OFF_CB = 0
OFF_CC = OFF_CB + CONV_WIDTH
OFF_XC = OFF_CC + CONV_WIDTH
OFF_ZC = OFF_XC + CONV_WIDTH
OFF_GA = OFF_ZC + CONV_WIDTH
OFF_GC = OFF_GA + D_MODEL
COLS_B = OFF_GC + D_MODEL
IN_COLS = COLS_A + GATE_RANK + COLS_B

_NT = (((1,), (1,)), ((), ()))
_TN = (((0,), (0,)), ((), ()))
F32 = jnp.float32


def _bf16(a):
    return a.astype(jnp.bfloat16)


def _rms(xf, g):
    ms = jnp.mean(xf * xf, axis=-1, keepdims=True)
    return xf * lax.rsqrt(ms + EPS) * g


def _sigmoid(a):
    return 0.5 * jnp.tanh(0.5 * a) + 0.5


def _log_sigmoid(a):
    return jnp.minimum(a, 0.0) - jnp.log(1.0 + jnp.exp(-jnp.abs(a)))


def _round_up(n, m):
    return (n + m - 1) // m * m


def _load_weights(layer, w_in_t_hbm, dense_hbm, gate_rows_ref, w_gk2_f32_ref,
                  w_in_ref, w_gkl_ref, w_gk2_ref, dense_refs, stage_ref, sem_ref):
    ahead = WSLOTS - 1
    n_in = (COLS_A + COLS_B) // WCHUNK
    dense = [(hbm, dst, r) for hbm, dst in zip(dense_hbm, dense_refs)
             for r in range(0, dst.shape[0], WCHUNK)]
    assert ahead <= len(dense) and ahead < n_in

    def copy(src, g):
        return pltpu.make_async_copy(src, stage_ref.at[g % WSLOTS], sem_ref.at[g % WSLOTS])

    def in_proj_copy(c):
        first = c * WCHUNK + jnp.where(c >= COLS_A // WCHUNK, GATE_RANK, 0)
        return copy(w_in_t_hbm.at[layer, pl.ds(first, WCHUNK), :], c)

    def dense_copy(i):
        hbm, _, r = dense[i]
        return copy(hbm.at[layer, pl.ds(r, WCHUNK), :], n_in + i)

    def in_proj_consume(c):
        w_in_ref[c] = _bf16(stage_ref[c % WSLOTS].T)

    for c in range(ahead):
        in_proj_copy(c).start()

    def body(c, carry):
        in_proj_copy(c).wait()
        in_proj_copy(c + ahead).start()
        in_proj_consume(c)
        return carry

    lax.fori_loop(0, n_in - ahead, body, 0)
    for c in range(n_in - ahead, n_in):
        in_proj_copy(c).wait()
        dense_copy(c + ahead - n_in).start()
        in_proj_consume(c)
    for i, (_, dst, r) in enumerate(dense):
        dense_copy(i).wait()
        if i + ahead < len(dense):
            dense_copy(i + ahead).start()
        dst[r:r + WCHUNK, :] = _bf16(stage_ref[(n_in + i) % WSLOTS])

    gate_rows = jnp.concatenate(
        [gate_rows_ref[...], jnp.zeros((LANE - GATE_RANK, D_MODEL), F32)], axis=0)
    w_gkl_ref[...] = _bf16(gate_rows.T)
    w_gk2_ref[...] = jnp.zeros_like(w_gk2_ref)
    w_gk2_ref[:GATE_RANK, :] = _bf16(w_gk2_f32_ref[...])


def _tile_stages(s, x_ref, p_ref, w_in_ref, w_gkl_ref, w_gk2_ref, w_a_ref,
                 w_c_ref, w_out_ref, w_pg_ref, w_pe_ref, g_mix_ref, b_gk_ref, g_gla_ref,
                 conv_w_ref, b_merge_ref, g_ple_ref, g_fin_ref, out_ref, state_ref, carry_ref,
                 apply_final_norm):
    ts = x_ref.shape[1]
    n_chunks = ts // GLA_CHUNK

    x = x_ref[s]
    inv = jnp.broadcast_to(
        lax.rsqrt(jnp.mean(x * x, axis=-1, keepdims=True) + EPS), (ts, D_MODEL))
    h = _bf16(x * g_mix_ref[...])
    yield

    def proj(first_chunk, off, width):
        return inv[:, :width] * jnp.concatenate(
            [jnp.dot(h, w_in_ref[first_chunk + c], preferred_element_type=F32)
             for c in range(off // WCHUNK, (off + width) // WCHUNK)], axis=1)

    proj_a = functools.partial(proj, 0)
    proj_b = functools.partial(proj, COLS_A // WCHUNK)

    gk_low = _bf16(inv[:, :LANE] * jnp.dot(h, w_gkl_ref[...], preferred_element_type=F32))
    q = proj_a(OFF_Q, GLA_KW)
    k = proj_a(OFF_K, GLA_KW)
    yield
    pre = jnp.dot(gk_low, w_gk2_ref[...], preferred_element_type=F32) + b_gk_ref[...]
    v = _bf16(proj_a(OFF_V, GLA_VW))
    yield
    z_a = proj_a(OFF_ZA, GLA_VW)
    gk = _log_sigmoid(pre) * (1.0 / GATE_NORMALIZER)
    act_a = _bf16(z_a * _sigmoid(z_a))
    yield
    m = proj_b(OFF_CC, CONV_WIDTH) * proj_b(OFF_XC, CONV_WIDTH)
    cum = gk
    shift = 1
    while shift < ts:
        cum = cum + jnp.concatenate(
            [jnp.zeros((shift, GLA_KW), F32), cum[:ts - shift, :]], axis=0)
        shift *= 2
    row = lax.broadcasted_iota(jnp.int32, (ts, ts), 0)
    col = lax.broadcasted_iota(jnp.int32, (ts, ts), 1)
    causal = col <= row
    yield
    cb = proj_b(OFF_CB, CONV_WIDTH)
    cum_end = cum[ts - 1:ts, :]
    starts = [jnp.zeros((1, GLA_KW), F32)]
    starts += [cum[c * GLA_CHUNK - 1:c * GLA_CHUNK, :] for c in range(1, n_chunks)]
    start_full = jnp.concatenate(
        [jnp.broadcast_to(st, (GLA_CHUNK, GLA_KW)) for st in starts], axis=0)
    exp_start_full = jnp.concatenate(
        [jnp.broadcast_to(jnp.exp(st), (GLA_CHUNK, GLA_KW)) for st in starts], axis=0)
    q_loc = q * (jnp.exp(cum - start_full) * (GLA_DK ** -0.5))
    q_in = _bf16(q_loc)
    q_tile = _bf16(q_loc * exp_start_full)
    k_end = _bf16(k * jnp.exp(cum_end - cum))
    k_rel = []
    for c in range(n_chunks):
        n = _round_up((c + 1) * GLA_CHUNK, LANE)
        k_rel.append(_bf16(k[:n, :] * jnp.exp(starts[c] - cum[:n, :])))
    tile_decay = jnp.exp(cum_end)
    ext = jnp.concatenate([carry_ref[s], m], axis=0)
    carry_ref[s] = m[ts - SUBLANE:, :]
    taps = [m] + [pltpu.roll(ext, j, 0)[SUBLANE:, :] for j in range(1, CONV_K)]
    cw = conv_w_ref[...]
    conv = cw[0:1, :] * taps[CONV_K - 1]
    for j in range(1, CONV_K):
        conv = conv + cw[j:j + 1, :] * taps[CONV_K - 1 - j]
    u = cb * conv
    yield

    attn, s_old = [], []
    for hd in range(GLA_HEADS):
        ks = slice(hd * GLA_DK, (hd + 1) * GLA_DK)
        vs = slice(hd * GLA_DV, (hd + 1) * GLA_DV)
        s_t = state_ref[s, hd]
        s_old.append(_bf16(s_t))
        blocks = []
        for c in range(n_chunks):
            rs = slice(c * GLA_CHUNK, (c + 1) * GLA_CHUNK)
            a_c = lax.dot_general(q_in[rs, ks], k_rel[c][:, ks], _NT,
                                  preferred_element_type=F32)
            if a_c.shape[1] < ts:
                a_c = jnp.concatenate(
                    [a_c, jnp.zeros((GLA_CHUNK, ts - a_c.shape[1]), F32)], axis=1)
            blocks.append(a_c)
        attn.append(_bf16(jnp.where(causal, jnp.concatenate(blocks, axis=0), 0.0)))
        state_ref[s, hd] = s_t * tile_decay[:, ks] + lax.dot_general(
            v[:, vs], k_end[:, ks], _TN, preferred_element_type=F32)
    z_c = proj_b(OFF_ZC, CONV_WIDTH)
    gated_c = _bf16(u * (z_c * _sigmoid(z_c)))
    yield
    g_gla = g_gla_ref[...]
    gated = []
    for hd in range(GLA_HEADS):
        ks = slice(hd * GLA_DK, (hd + 1) * GLA_DK)
        vs = slice(hd * GLA_DV, (hd + 1) * GLA_DV)
        o = (lax.dot_general(q_tile[:, ks], s_old[hd], _NT, preferred_element_type=F32)
             + jnp.dot(attn[hd], v[:, vs], preferred_element_type=F32))
        gated.append(_bf16(_rms(o, g_gla) * act_a[:, vs]))
    b_merge = b_merge_ref[...]
    gate_c = _bf16(_sigmoid(proj_b(OFF_GC, D_MODEL) + b_merge[:, D_MODEL:]))
    yield
    y_c = _bf16(gate_c * jnp.dot(gated_c, w_c_ref[...], preferred_element_type=F32))
    gate_a = _bf16(_sigmoid(proj_b(OFF_GA, D_MODEL) + b_merge[:, :D_MODEL]))
    yield
    y_a = jnp.dot(jnp.concatenate(gated, axis=1), w_a_ref[...], preferred_element_type=F32)
    merged = _bf16(gate_a * y_a + y_c)
    yield

    x1 = x + jnp.dot(merged, w_out_ref[...], preferred_element_type=F32)
    yield
    inv1 = lax.rsqrt(jnp.mean(x1 * x1, axis=-1, keepdims=True) + EPS)
    hn = _bf16(x1 * g_ple_ref[...])
    yield
    ple_gate = _sigmoid(inv1 * jnp.dot(hn, w_pg_ref[...], preferred_element_type=F32))
    x2 = x1 + ple_gate * jnp.dot(_bf16(p_ref[s]), w_pe_ref[...], preferred_element_type=F32)
    yield
    if apply_final_norm:
        x2 = _rms(x2, g_fin_ref[...])
    out_ref[s] = x2


def _layer_kernel(x_ref, p_ref, w_in_t_hbm, wa_hbm, wc_hbm, wo_hbm, wpg_hbm, wpe_hbm,
                  gate_rows_ref, w_gk2_f32_ref, g_mix_ref, b_gk_ref, g_gla_ref, conv_w_ref,
                  b_merge_ref, g_ple_ref, g_fin_ref, out_ref,
                  w_in_ref, w_gkl_ref, w_gk2_ref, w_a_ref, w_c_ref, w_out_ref,
                  w_pg_ref, w_pe_ref, stage_ref, sem_ref, state_ref, carry_ref,
                  *, layer, apply_final_norm):
    @pl.when((pl.program_id(0) == 0) & (pl.program_id(1) == 0))
    def _():
        _load_weights(layer, w_in_t_hbm, (wa_hbm, wc_hbm, wo_hbm, wpg_hbm, wpe_hbm),
                      gate_rows_ref, w_gk2_f32_ref, w_in_ref, w_gkl_ref, w_gk2_ref,
                      (w_a_ref, w_c_ref, w_out_ref, w_pg_ref, w_pe_ref), stage_ref, sem_ref)

    @pl.when(pl.program_id(1) == 0)
    def _():
        state_ref[...] = jnp.zeros_like(state_ref)
        carry_ref[...] = jnp.zeros_like(carry_ref)

    live = [_tile_stages(s, x_ref, p_ref, w_in_ref, w_gkl_ref, w_gk2_ref, w_a_ref,
                         w_c_ref, w_out_ref, w_pg_ref, w_pe_ref, g_mix_ref, b_gk_ref, g_gla_ref,
                         conv_w_ref, b_merge_ref, g_ple_ref, g_fin_ref, out_ref, state_ref,
                         carry_ref, apply_final_norm)
            for s in range(x_ref.shape[0])]
    while live:
        for g in list(live):
            try:
                next(g)
            except StopIteration:
                live.remove(g)


def _const_spec(shape, index=None):
    index = (0,) * len(shape) if index is None else index
    return pl.BlockSpec(shape, lambda b, t: index, pipeline_mode=pl.Buffered(1))


def _layer(layer, x, p_i, w_in, w_gk2, big_weights, vectors, *, apply_final_norm):
    bsz, seq, d = x.shape
    assert d == D_MODEL and seq % SEQ_TILE == 0 and SEQ_TILE % LANE == 0
    assert w_in.shape[1:] == (D_MODEL, IN_COLS) and COLS_A % GATE_RANK == 0
    assert CONV_K - 1 <= SUBLANE
    nb = 2 if bsz % 2 == 0 else 1
    w_in_t = jnp.swapaxes(w_in, 1, 2)
    tile = lambda width: pl.BlockSpec((nb, SEQ_TILE, width), lambda b, t: (b, t, 0))
    in_hbm = pl.BlockSpec(memory_space=pltpu.HBM)
    in_specs = ([tile(D_MODEL), tile(PLE_DIM)] + [in_hbm] * (1 + len(big_weights))
                + [_const_spec((None, GATE_RANK, D_MODEL), (layer, COLS_A // GATE_RANK, 0)),
                   _const_spec((None, GATE_RANK, GLA_KW), (layer, 0, 0))]
                + [_const_spec(vv.shape) if vv.ndim == 2
                   else _const_spec((None,) + vv.shape[1:], (layer, 0, 0)) for vv in vectors])
    bf16 = jnp.bfloat16
    return pl.pallas_call(
        functools.partial(_layer_kernel, layer=layer, apply_final_norm=apply_final_norm),
        grid=(bsz // nb, seq // SEQ_TILE),
        in_specs=in_specs,
        out_specs=tile(D_MODEL),
        out_shape=jax.ShapeDtypeStruct(x.shape, jnp.float32),
        scratch_shapes=[
            pltpu.VMEM(((COLS_A + COLS_B) // WCHUNK, D_MODEL, WCHUNK), bf16),
            pltpu.VMEM((D_MODEL, LANE), bf16),
            pltpu.VMEM((LANE, GLA_KW), bf16),
            pltpu.VMEM((GLA_VW, D_MODEL), bf16),
            pltpu.VMEM((CONV_WIDTH, D_MODEL), bf16),
            pltpu.VMEM((D_MODEL, D_MODEL), bf16),
            pltpu.VMEM((D_MODEL, D_MODEL), bf16),
            pltpu.VMEM((PLE_DIM, D_MODEL), bf16),
            pltpu.VMEM((WSLOTS, WCHUNK, D_MODEL), jnp.float32),
            pltpu.SemaphoreType.DMA((WSLOTS,)),
            pltpu.VMEM((nb, GLA_HEADS, GLA_DV, GLA_DK), jnp.float32),
            pltpu.VMEM((nb, SUBLANE, CONV_WIDTH), jnp.float32),
        ],
        compiler_params=pltpu.CompilerParams(
            dimension_semantics=("arbitrary", "arbitrary"),
            vmem_limit_bytes=VMEM_LIMIT_BYTES),
        name="hybrid_layer",
    )(x, p_i, w_in_t, *big_weights, w_in_t, w_gk2, *vectors)


def kernel(x, p, norm_mix_g, w_in, b_merge, w_gk2, b_gk, gla_norm_g, conv_w, w_branch_a,
           w_branch_c, w_out, norm_ple_g, w_ple_gate, w_ple_proj, norm_final_g):
    depth = w_in.shape[0]
    row = lambda a: a.reshape(1, -1).astype(jnp.float32)
    for i in range(depth):
        vectors = [row(norm_mix_g[i]), row(b_gk[i]), row(gla_norm_g[i]),
                   conv_w.astype(jnp.float32), row(b_merge[i]), row(norm_ple_g[i]),
                   row(norm_final_g)]
        x = _layer(i, x, p[i], w_in, w_gk2,
                   [w_branch_a, w_branch_c, w_out, w_ple_gate, w_ple_proj], vectors,
                   apply_final_norm=(i == depth - 1))
    return x
```

```python
import functools

import jax
import jax.numpy as jnp
from jax import lax
from jax.experimental import pallas as pl
from jax.experimental.pallas import tpu as pltpu

D_MODEL = 1024
GLA_HEADS = 4
GLA_DK = 128
GLA_DV = 256
GLA_KW = GLA_HEADS * GLA_DK
GLA_VW = GLA_HEADS * GLA_DV
GATE_RANK = 16
GATE_NORMALIZER = 16.0
GLA_CHUNK = 64
CONV_WIDTH = 1024
CONV_K = 3
PLE_DIM = 256
EPS = 1e-6

LANE = 128
SUBLANE = 8
SEQ_TILE = 256
WCHUNK = 256
WSLOTS = 4
VMEM_LIMIT_BYTES = 58 * 1024 * 1024

OFF_Q = 0
OFF_K = OFF_Q + GLA_KW
OFF_V = OFF_K + GLA_KW
OFF_ZA = OFF_V + GLA_VW
COLS_A = OFF_ZA + GLA_VW
OFF_CB = 0
OFF_CC = OFF_CB + CONV_WIDTH
OFF_XC = OFF_CC + CONV_WIDTH
OFF_ZC = OFF_XC + CONV_WIDTH
OFF_GA = OFF_ZC + CONV_WIDTH
OFF_GC = OFF_GA + D_MODEL
COLS_B = OFF_GC + D_MODEL
IN_COLS = COLS_A + GATE_RANK + COLS_B

_NT = (((1,), (1,)), ((), ()))
_TN = (((0,), (0,)), ((), ()))
F32 = jnp.float32


def _bf16(a):
    return a.astype(jnp.bfloat16)


def _rms(xf, g):
    ms = jnp.mean(xf * xf, axis=-1, keepdims=True)
    return xf * lax.rsqrt(ms + EPS) * g


def _sigmoid(a):
    return 0.5 * jnp.tanh(0.5 * a) + 0.5


def _log_sigmoid(a):
    return jnp.minimum(a, 0.0) - jnp.log(1.0 + jnp.exp(-jnp.abs(a)))


def _round_up(n, m):
    return (n + m - 1) // m * m


def _load_weights(layer, w_in_t_hbm, dense_hbm, gate_rows_ref, w_gk2_f32_ref,
                  w_in_ref, w_gkl_ref, w_gk2_ref, dense_refs, stage_ref, sem_ref):
    ahead = WSLOTS - 1
    n_in = (COLS_A + COLS_B) // WCHUNK
    dense = [(hbm, dst, r) for hbm, dst in zip(dense_hbm, dense_refs)
             for r in range(0, dst.shape[0], WCHUNK)]
    assert ahead <= len(dense) and ahead < n_in

    def copy(src, g):
        return pltpu.make_async_copy(src, stage_ref.at[g % WSLOTS], sem_ref.at[g % WSLOTS])

    def in_proj_copy(c):
        first = c * WCHUNK + jnp.where(c >= COLS_A // WCHUNK, GATE_RANK, 0)
        return copy(w_in_t_hbm.at[layer, pl.ds(first, WCHUNK), :], c)

    def dense_copy(i):
        hbm, _, r = dense[i]
        return copy(hbm.at[layer, pl.ds(r, WCHUNK), :], n_in + i)

    def in_proj_consume(c):
        w_in_ref[c] = _bf16(stage_ref[c % WSLOTS].T)

    for c in range(ahead):
        in_proj_copy(c).start()

    def body(c, carry):
        in_proj_copy(c).wait()
        in_proj_copy(c + ahead).start()
        in_proj_consume(c)
        return carry

    lax.fori_loop(0, n_in - ahead, body, 0)
    for c in range(n_in - ahead, n_in):
        in_proj_copy(c).wait()
        dense_copy(c + ahead - n_in).start()
        in_proj_consume(c)
    for i, (_, dst, r) in enumerate(dense):
        dense_copy(i).wait()
        if i + ahead < len(dense):
            dense_copy(i + ahead).start()
        dst[r:r + WCHUNK, :] = _bf16(stage_ref[(n_in + i) % WSLOTS])

    gate_rows = jnp.concatenate(
        [gate_rows_ref[...], jnp.zeros((LANE - GATE_RANK, D_MODEL), F32)], axis=0)
    w_gkl_ref[...] = _bf16(gate_rows.T)
    w_gk2_ref[...] = jnp.zeros_like(w_gk2_ref)
    w_gk2_ref[:GATE_RANK, :] = _bf16(w_gk2_f32_ref[...])


def _tile_stages(s, x_ref, p_ref, w_in_ref, w_gkl_ref, w_gk2_ref, w_a_ref,
                 w_c_ref, w_out_ref, w_pg_ref, w_pe_ref, g_mix_ref, b_gk_ref, g_gla_ref,
                 conv_w_ref, b_merge_ref, g_ple_ref, g_fin_ref, out_ref, state_ref, carry_ref,
                 apply_final_norm):
    ts = x_ref.shape[1]
    n_chunks = ts // GLA_CHUNK

    x = x_ref[s]
    inv = jnp.broadcast_to(
        lax.rsqrt(jnp.mean(x * x, axis=-1, keepdims=True) + EPS), (ts, D_MODEL))
    h = _bf16(x * g_mix_ref[...])
    yield

    def proj(first_chunk, off, width):
        return inv[:, :width] * jnp.concatenate(
            [jnp.dot(h, w_in_ref[first_chunk + c], preferred_element_type=F32)
             for c in range(off // WCHUNK, (off + width) // WCHUNK)], axis=1)

    proj_a = functools.partial(proj, 0)
    proj_b = functools.partial(proj, COLS_A // WCHUNK)

    gk_low = _bf16(inv[:, :LANE] * jnp.dot(h, w_gkl_ref[...], preferred_element_type=F32))
    q = proj_a(OFF_Q, GLA_KW)
    k = proj_a(OFF_K, GLA_KW)
    yield
    pre = jnp.dot(gk_low, w_gk2_ref[...], preferred_element_type=F32) + b_gk_ref[...]
    v = _bf16(proj_a(OFF_V, GLA_VW))
    yield
    z_a = proj_a(OFF_ZA, GLA_VW)
    gk = _log_sigmoid(pre) * (1.0 / GATE_NORMALIZER)
    act_a = _bf16(z_a * _sigmoid(z_a))
    yield
    m = proj_b(OFF_CC, CONV_WIDTH) * proj_b(OFF_XC, CONV_WIDTH)
    cum = gk
    shift = 1
    while shift < ts:
        cum = cum + jnp.concatenate(
            [jnp.zeros((shift, GLA_KW), F32), cum[:ts - shift, :]], axis=0)
        shift *= 2
    row = lax.broadcasted_iota(jnp.int32, (ts, ts), 0)
    col = lax.broadcasted_iota(jnp.int32, (ts, ts), 1)
    causal = col <= row
    yield
    cb = proj_b(OFF_CB, CONV_WIDTH)
    cum_end = cum[ts - 1:ts, :]
    starts = [jnp.zeros((1, GLA_KW), F32)]
    starts += [cum[c * GLA_CHUNK - 1:c * GLA_CHUNK, :] for c in range(1, n_chunks)]
    start_full = jnp.concatenate(
        [jnp.broadcast_to(st, (GLA_CHUNK, GLA_KW)) for st in starts], axis=0)
    exp_start_full = jnp.concatenate(
        [jnp.broadcast_to(jnp.exp(st), (GLA_CHUNK, GLA_KW)) for st in starts], axis=0)
    q_loc = q * (jnp.exp(cum - start_full) * (GLA_DK ** -0.5))
    q_in = _bf16(q_loc)
    q_tile = _bf16(q_loc * exp_start_full)
    k_end = _bf16(k * jnp.exp(cum_end - cum))
    k_rel = []
    for c in range(n_chunks):
        n = _round_up((c + 1) * GLA_CHUNK, LANE)
        k_rel.append(_bf16(k[:n, :] * jnp.exp(starts[c] - cum[:n, :])))
    tile_decay = jnp.exp(cum_end)
    ext = jnp.concatenate([carry_ref[...], m], axis=0)
    carry_ref[...] = m[ts - SUBLANE:, :]
    taps = [m] + [pltpu.roll(ext, j, 0)[SUBLANE:, :] for j in range(1, CONV_K)]
    cw = conv_w_ref[...]
    conv = cw[0:1, :] * taps[CONV_K - 1]
    for j in range(1, CONV_K):
        conv = conv + cw[j:j + 1, :] * taps[CONV_K - 1 - j]
    u = cb * conv
    yield

    attn, s_old = [], []
    for hd in range(GLA_HEADS):
        ks = slice(hd * GLA_DK, (hd + 1) * GLA_DK)
        vs = slice(hd * GLA_DV, (hd + 1) * GLA_DV)
        s_t = state_ref[hd]
        s_old.append(_bf16(s_t))
        blocks = []
        for c in range(n_chunks):
            rs = slice(c * GLA_CHUNK, (c + 1) * GLA_CHUNK)
            a_c = lax.dot_general(q_in[rs, ks], k_rel[c][:, ks], _NT,
                                  preferred_element_type=F32)
            if a_c.shape[1] < ts:
                a_c = jnp.concatenate(
                    [a_c, jnp.zeros((GLA_CHUNK, ts - a_c.shape[1]), F32)], axis=1)
            blocks.append(a_c)
        attn.append(_bf16(jnp.where(causal, jnp.concatenate(blocks, axis=0), 0.0)))
        state_ref[hd] = s_t * tile_decay[:, ks] + lax.dot_general(
            v[:, vs], k_end[:, ks], _TN, preferred_element_type=F32)
    z_c = proj_b(OFF_ZC, CONV_WIDTH)
    gated_c = _bf16(u * (z_c * _sigmoid(z_c)))
    yield
    g_gla = g_gla_ref[...]
    gated = []
    for hd in range(GLA_HEADS):
        ks = slice(hd * GLA_DK, (hd + 1) * GLA_DK)
        vs = slice(hd * GLA_DV, (hd + 1) * GLA_DV)
        o = (lax.dot_general(q_tile[:, ks], s_old[hd], _NT, preferred_element_type=F32)
             + jnp.dot(attn[hd], v[:, vs], preferred_element_type=F32))
        gated.append(_bf16(_rms(o, g_gla) * act_a[:, vs]))
    b_merge = b_merge_ref[...]
    gate_c = _bf16(_sigmoid(proj_b(OFF_GC, D_MODEL) + b_merge[:, D_MODEL:]))
    yield
    y_c = _bf16(gate_c * jnp.dot(gated_c, w_c_ref[...], preferred_element_type=F32))
    gate_a = _bf16(_sigmoid(proj_b(OFF_GA, D_MODEL) + b_merge[:, :D_MODEL]))
    yield
    y_a = jnp.dot(jnp.concatenate(gated, axis=1), w_a_ref[...], preferred_element_type=F32)
    merged = _bf16(gate_a * y_a + y_c)
    yield

    x1 = x + jnp.dot(merged, w_out_ref[...], preferred_element_type=F32)
    yield
    inv1 = lax.rsqrt(jnp.mean(x1 * x1, axis=-1, keepdims=True) + EPS)
    hn = _bf16(x1 * g_ple_ref[...])
    yield
    ple_gate = _sigmoid(inv1 * jnp.dot(hn, w_pg_ref[...], preferred_element_type=F32))
    x2 = x1 + ple_gate * jnp.dot(_bf16(p_ref[s]), w_pe_ref[...], preferred_element_type=F32)
    yield
    if apply_final_norm:
        x2 = _rms(x2, g_fin_ref[...])
    out_ref[s] = x2


def _layer_kernel(x_ref, p_ref, w_in_t_hbm, wa_hbm, wc_hbm, wo_hbm, wpg_hbm, wpe_hbm,
                  gate_rows_ref, w_gk2_f32_ref, g_mix_ref, b_gk_ref, g_gla_ref, conv_w_ref,
                  b_merge_ref, g_ple_ref, g_fin_ref, out_ref,
                  w_in_ref, w_gkl_ref, w_gk2_ref, w_a_ref, w_c_ref, w_out_ref,
                  w_pg_ref, w_pe_ref, stage_ref, sem_ref, state_ref, carry_ref,
                  *, layer, apply_final_norm):
    @pl.when((pl.program_id(0) == 0) & (pl.program_id(1) == 0))
    def _():
        _load_weights(layer, w_in_t_hbm, (wa_hbm, wc_hbm, wo_hbm, wpg_hbm, wpe_hbm),
                      gate_rows_ref, w_gk2_f32_ref, w_in_ref, w_gkl_ref, w_gk2_ref,
                      (w_a_ref, w_c_ref, w_out_ref, w_pg_ref, w_pe_ref), stage_ref, sem_ref)

    @pl.when(pl.program_id(1) == 0)
    def _():
        state_ref[...] = jnp.zeros_like(state_ref)
        carry_ref[...] = jnp.zeros_like(carry_ref)

    live = [_tile_stages(s, x_ref, p_ref, w_in_ref, w_gkl_ref, w_gk2_ref, w_a_ref,
                         w_c_ref, w_out_ref, w_pg_ref, w_pe_ref, g_mix_ref, b_gk_ref, g_gla_ref,
                         conv_w_ref, b_merge_ref, g_ple_ref, g_fin_ref, out_ref, state_ref,
                         carry_ref, apply_final_norm)
            for s in range(x_ref.shape[0])]
    while live:
        for g in list(live):
            try:
                next(g)
            except StopIteration:
                live.remove(g)


def _const_spec(shape, index=None):
    index = (0,) * len(shape) if index is None else index
    return pl.BlockSpec(shape, lambda b, t: index, pipeline_mode=pl.Buffered(1))


def _layer(layer, x, p_i, w_in, w_gk2, big_weights, vectors, *, apply_final_norm):
    bsz, seq, d = x.shape
    assert d == D_MODEL and seq % SEQ_TILE == 0 and SEQ_TILE % LANE == 0
    assert w_in.shape[1:] == (D_MODEL, IN_COLS) and COLS_A % GATE_RANK == 0
    assert CONV_K - 1 <= SUBLANE
    nt = 2 if (seq // SEQ_TILE) % 2 == 0 else 1
    x = x.reshape(bsz, seq // SEQ_TILE, SEQ_TILE, d)
    p_i = p_i.reshape(bsz, seq // SEQ_TILE, SEQ_TILE, PLE_DIM)
    w_in_t = jnp.swapaxes(w_in, 1, 2)
    tile = lambda width: pl.BlockSpec((None, nt, SEQ_TILE, width), lambda b, t: (b, t, 0, 0))
    in_hbm = pl.BlockSpec(memory_space=pltpu.HBM)
    in_specs = ([tile(D_MODEL), tile(PLE_DIM)] + [in_hbm] * (1 + len(big_weights))
                + [_const_spec((None, GATE_RANK, D_MODEL), (layer, COLS_A // GATE_RANK, 0)),
                   _const_spec((None, GATE_RANK, GLA_KW), (layer, 0, 0))]
                + [_const_spec(vv.shape) if vv.ndim == 2
                   else _const_spec((None,) + vv.shape[1:], (layer, 0, 0)) for vv in vectors])
    bf16 = jnp.bfloat16
    return pl.pallas_call(
        functools.partial(_layer_kernel, layer=layer, apply_final_norm=apply_final_norm),
        grid=(bsz, seq // SEQ_TILE // nt),
        in_specs=in_specs,
        out_specs=tile(D_MODEL),
        out_shape=jax.ShapeDtypeStruct(x.shape, jnp.float32),
        scratch_shapes=[
            pltpu.VMEM(((COLS_A + COLS_B) // WCHUNK, D_MODEL, WCHUNK), bf16),
            pltpu.VMEM((D_MODEL, LANE), bf16),
            pltpu.VMEM((LANE, GLA_KW), bf16),
            pltpu.VMEM((GLA_VW, D_MODEL), bf16),
            pltpu.VMEM((CONV_WIDTH, D_MODEL), bf16),
            pltpu.VMEM((D_MODEL, D_MODEL), bf16),
            pltpu.VMEM((D_MODEL, D_MODEL), bf16),
            pltpu.VMEM((PLE_DIM, D_MODEL), bf16),
            pltpu.VMEM((WSLOTS, WCHUNK, D_MODEL), jnp.float32),
            pltpu.SemaphoreType.DMA((WSLOTS,)),
            pltpu.VMEM((GLA_HEADS, GLA_DV, GLA_DK), jnp.float32),
            pltpu.VMEM((SUBLANE, CONV_WIDTH), jnp.float32),
        ],
        compiler_params=pltpu.CompilerParams(
            dimension_semantics=("arbitrary", "arbitrary"),
            vmem_limit_bytes=VMEM_LIMIT_BYTES),
        name="hybrid_layer",
    )(x, p_i, w_in_t, *big_weights, w_in_t, w_gk2, *vectors).reshape(bsz, seq, d)


def kernel(x, p, norm_mix_g, w_in, b_merge, w_gk2, b_gk, gla_norm_g, conv_w, w_branch_a,
           w_branch_c, w_out, norm_ple_g, w_ple_gate, w_ple_proj, norm_final_g):
    depth = w_in.shape[0]
    row = lambda a: a.reshape(1, -1).astype(jnp.float32)
    for i in range(depth):
        vectors = [row(norm_mix_g[i]), row(b_gk[i]), row(gla_norm_g[i]),
                   conv_w.astype(jnp.float32), row(b_merge[i]), row(norm_ple_g[i]),
                   row(norm_final_g)]
        x = _layer(i, x, p[i], w_in, w_gk2,
                   [w_branch_a, w_branch_c, w_out, w_ple_gate, w_ple_proj], vectors,
                   apply_final_norm=(i == depth - 1))
    return x
```

```python
import functools

import jax
import jax.numpy as jnp
from jax import lax
from jax.experimental import pallas as pl
from jax.experimental.pallas import tpu as pltpu

D_MODEL = 1024
GLA_HEADS = 4
GLA_DK = 128
GLA_DV = 256
GLA_KW = GLA_HEADS * GLA_DK
GLA_VW = GLA_HEADS * GLA_DV
GATE_RANK = 16
GATE_NORMALIZER = 16.0
GLA_CHUNK = 64
CONV_WIDTH = 1024
CONV_K = 3
PLE_DIM = 256
EPS = 1e-6

LANE = 128
SUBLANE = 8
SEQ_TILE = 256
WCHUNK = 256
TAIL_TILE = 1024
TAIL_ROWS = 256
WSLOTS = 4
VMEM_LIMIT_BYTES = 58 * 1024 * 1024

OFF_Q = 0
OFF_K = OFF_Q + GLA_KW
OFF_V = OFF_K + GLA_KW
OFF_ZA = OFF_V + GLA_VW
COLS_A = OFF_ZA + GLA_VW
OFF_CB = 0
OFF_CC = OFF_CB + CONV_WIDTH
OFF_XC = OFF_CC + CONV_WIDTH
OFF_ZC = OFF_XC + CONV_WIDTH
OFF_GA = OFF_ZC + CONV_WIDTH
OFF_GC = OFF_GA + D_MODEL
COLS_B = OFF_GC + D_MODEL
IN_COLS = COLS_A + GATE_RANK + COLS_B

_NT = (((1,), (1,)), ((), ()))
_TN = (((0,), (0,)), ((), ()))
F32 = jnp.float32


def _bf16(a):
    return a.astype(jnp.bfloat16)


def _rms(xf, g):
    ms = jnp.mean(xf * xf, axis=-1, keepdims=True)
    return xf * lax.rsqrt(ms + EPS) * g


def _sigmoid(a):
    return 0.5 * jnp.tanh(0.5 * a) + 0.5


def _log_sigmoid(a):
    return jnp.minimum(a, 0.0) - jnp.log(1.0 + jnp.exp(-jnp.abs(a)))


def _round_up(n, m):
    return (n + m - 1) // m * m


def _load_weights(layer, w_in_t_hbm, dense_hbm, gate_rows_ref, w_gk2_f32_ref,
                  w_in_ref, w_gkl_ref, w_gk2_ref, dense_refs, stage_ref, sem_ref):
    ahead = WSLOTS - 1
    n_in = (COLS_A + COLS_B) // WCHUNK
    dense = [(hbm, dst, r) for hbm, dst in zip(dense_hbm, dense_refs)
             for r in range(0, dst.shape[0], WCHUNK)]
    assert ahead <= len(dense) and ahead < n_in

    def copy(src, g):
        return pltpu.make_async_copy(src, stage_ref.at[g % WSLOTS], sem_ref.at[g % WSLOTS])

    def in_proj_copy(c):
        first = c * WCHUNK + jnp.where(c >= COLS_A // WCHUNK, GATE_RANK, 0)
        return copy(w_in_t_hbm.at[layer, pl.ds(first, WCHUNK), :], c)

    def dense_copy(i):
        hbm, _, r = dense[i]
        return copy(hbm.at[layer, pl.ds(r, WCHUNK), :], n_in + i)

    def in_proj_consume(c):
        w_in_ref[c] = _bf16(stage_ref[c % WSLOTS].T)

    for c in range(ahead):
        in_proj_copy(c).start()

    def body(c, carry):
        in_proj_copy(c).wait()
        in_proj_copy(c + ahead).start()
        in_proj_consume(c)
        return carry

    lax.fori_loop(0, n_in - ahead, body, 0)
    for c in range(n_in - ahead, n_in):
        in_proj_copy(c).wait()
        dense_copy(c + ahead - n_in).start()
        in_proj_consume(c)
    for i, (_, dst, r) in enumerate(dense):
        dense_copy(i).wait()
        if i + ahead < len(dense):
            dense_copy(i + ahead).start()
        dst[r:r + WCHUNK, :] = _bf16(stage_ref[(n_in + i) % WSLOTS])

    gate_rows = jnp.concatenate(
        [gate_rows_ref[...], jnp.zeros((LANE - GATE_RANK, D_MODEL), F32)], axis=0)
    w_gkl_ref[...] = _bf16(gate_rows.T)
    w_gk2_ref[...] = jnp.zeros_like(w_gk2_ref)
    w_gk2_ref[:GATE_RANK, :] = _bf16(w_gk2_f32_ref[...])


def _tile_stages(s, x_ref, w_in_ref, w_gkl_ref, w_gk2_ref, w_a_ref, w_c_ref, g_mix_ref, b_gk_ref,
                 g_gla_ref, conv_w_ref, b_merge_ref, merged_ref, state_ref, carry_ref):
    ts = x_ref.shape[1]
    n_chunks = ts // GLA_CHUNK

    x = x_ref[s]
    inv = jnp.broadcast_to(
        lax.rsqrt(jnp.mean(x * x, axis=-1, keepdims=True) + EPS), (ts, D_MODEL))
    h = _bf16(x * g_mix_ref[...])
    yield

    def proj(first_chunk, off, width):
        return inv[:, :width] * jnp.concatenate(
            [jnp.dot(h, w_in_ref[first_chunk + c], preferred_element_type=F32)
             for c in range(off // WCHUNK, (off + width) // WCHUNK)], axis=1)

    proj_a = functools.partial(proj, 0)
    proj_b = functools.partial(proj, COLS_A // WCHUNK)

    gk_low = _bf16(inv[:, :LANE] * jnp.dot(h, w_gkl_ref[...], preferred_element_type=F32))
    q = proj_a(OFF_Q, GLA_KW)
    k = proj_a(OFF_K, GLA_KW)
    yield
    pre = jnp.dot(gk_low, w_gk2_ref[...], preferred_element_type=F32) + b_gk_ref[...]
    v = _bf16(proj_a(OFF_V, GLA_VW))
    yield
    z_a = proj_a(OFF_ZA, GLA_VW)
    gk = _log_sigmoid(pre) * (1.0 / GATE_NORMALIZER)
    act_a = _bf16(z_a * _sigmoid(z_a))
    yield
    m = proj_b(OFF_CC, CONV_WIDTH) * proj_b(OFF_XC, CONV_WIDTH)
    cum = gk
    shift = 1
    while shift < ts:
        cum = cum + jnp.concatenate(
            [jnp.zeros((shift, GLA_KW), F32), cum[:ts - shift, :]], axis=0)
        shift *= 2
    row = lax.broadcasted_iota(jnp.int32, (ts, ts), 0)
    col = lax.broadcasted_iota(jnp.int32, (ts, ts), 1)
    causal = col <= row
    yield
    cb = proj_b(OFF_CB, CONV_WIDTH)
    cum_end = cum[ts - 1:ts, :]
    starts = [jnp.zeros((1, GLA_KW), F32)]
    starts += [cum[c * GLA_CHUNK - 1:c * GLA_CHUNK, :] for c in range(1, n_chunks)]
    start_full = jnp.concatenate(
        [jnp.broadcast_to(st, (GLA_CHUNK, GLA_KW)) for st in starts], axis=0)
    exp_start_full = jnp.concatenate(
        [jnp.broadcast_to(jnp.exp(st), (GLA_CHUNK, GLA_KW)) for st in starts], axis=0)
    q_loc = q * (jnp.exp(cum - start_full) * (GLA_DK ** -0.5))
    q_in = _bf16(q_loc)
    q_tile = _bf16(q_loc * exp_start_full)
    k_end = _bf16(k * jnp.exp(cum_end - cum))
    k_rel = []
    for c in range(n_chunks):
        n = _round_up((c + 1) * GLA_CHUNK, LANE)
        k_rel.append(_bf16(k[:n, :] * jnp.exp(starts[c] - cum[:n, :])))
    tile_decay = jnp.exp(cum_end)
    ext = jnp.concatenate([carry_ref[s], m], axis=0)
    carry_ref[s] = m[ts - SUBLANE:, :]
    taps = [m] + [pltpu.roll(ext, j, 0)[SUBLANE:, :] for j in range(1, CONV_K)]
    cw = conv_w_ref[...]
    conv = cw[0:1, :] * taps[CONV_K - 1]
    for j in range(1, CONV_K):
        conv = conv + cw[j:j + 1, :] * taps[CONV_K - 1 - j]
    u = cb * conv
    yield

    attn, s_old = [], []
    for hd in range(GLA_HEADS):
        ks = slice(hd * GLA_DK, (hd + 1) * GLA_DK)
        vs = slice(hd * GLA_DV, (hd + 1) * GLA_DV)
        s_t = state_ref[s, hd]
        s_old.append(_bf16(s_t))
        blocks = []
        for c in range(n_chunks):
            rs = slice(c * GLA_CHUNK, (c + 1) * GLA_CHUNK)
            a_c = lax.dot_general(q_in[rs, ks], k_rel[c][:, ks], _NT,
                                  preferred_element_type=F32)
            if a_c.shape[1] < ts:
                a_c = jnp.concatenate(
                    [a_c, jnp.zeros((GLA_CHUNK, ts - a_c.shape[1]), F32)], axis=1)
            blocks.append(a_c)
        attn.append(_bf16(jnp.where(causal, jnp.concatenate(blocks, axis=0), 0.0)))
        state_ref[s, hd] = s_t * tile_decay[:, ks] + lax.dot_general(
            v[:, vs], k_end[:, ks], _TN, preferred_element_type=F32)
    z_c = proj_b(OFF_ZC, CONV_WIDTH)
    gated_c = _bf16(u * (z_c * _sigmoid(z_c)))
    yield
    g_gla = g_gla_ref[...]
    gated = []
    for hd in range(GLA_HEADS):
        ks = slice(hd * GLA_DK, (hd + 1) * GLA_DK)
        vs = slice(hd * GLA_DV, (hd + 1) * GLA_DV)
        o = (lax.dot_general(q_tile[:, ks], s_old[hd], _NT, preferred_element_type=F32)
             + jnp.dot(attn[hd], v[:, vs], preferred_element_type=F32))
        gated.append(_bf16(_rms(o, g_gla) * act_a[:, vs]))
    b_merge = b_merge_ref[...]
    gate_c = _bf16(_sigmoid(proj_b(OFF_GC, D_MODEL) + b_merge[:, D_MODEL:]))
    yield
    y_c = _bf16(gate_c * jnp.dot(gated_c, w_c_ref[...], preferred_element_type=F32))
    gate_a = _bf16(_sigmoid(proj_b(OFF_GA, D_MODEL) + b_merge[:, :D_MODEL]))
    yield
    y_a = jnp.dot(jnp.concatenate(gated, axis=1), w_a_ref[...], preferred_element_type=F32)
    merged = _bf16(gate_a * y_a + y_c)

    merged_ref[s] = merged


def _layer_kernel(x_ref, w_in_t_hbm, wa_hbm, wc_hbm, gate_rows_ref, w_gk2_f32_ref, g_mix_ref,
                  b_gk_ref, g_gla_ref, conv_w_ref, b_merge_ref, merged_ref,
                  w_in_ref, w_gkl_ref, w_gk2_ref, w_a_ref, w_c_ref, stage_ref, sem_ref,
                  state_ref, carry_ref, *, layer):
    @pl.when((pl.program_id(0) == 0) & (pl.program_id(1) == 0))
    def _():
        _load_weights(layer, w_in_t_hbm, (wa_hbm, wc_hbm), gate_rows_ref, w_gk2_f32_ref,
                      w_in_ref, w_gkl_ref, w_gk2_ref, (w_a_ref, w_c_ref), stage_ref, sem_ref)

    @pl.when(pl.program_id(1) == 0)
    def _():
        state_ref[...] = jnp.zeros_like(state_ref)
        carry_ref[...] = jnp.zeros_like(carry_ref)

    live = [_tile_stages(s, x_ref, w_in_ref, w_gkl_ref, w_gk2_ref, w_a_ref, w_c_ref, g_mix_ref,
                         b_gk_ref, g_gla_ref, conv_w_ref, b_merge_ref, merged_ref, state_ref,
                         carry_ref)
            for s in range(x_ref.shape[0])]
    while live:
        for g in list(live):
            try:
                next(g)
            except StopIteration:
                live.remove(g)


def _tail_stages(rows, x_ref, merged_ref, p_ref, w_out_ref, w_pg_ref, w_pe_ref, g_ple_ref,
                 g_fin_ref, out_ref, apply_final_norm):
    x1 = x_ref[rows, :] + jnp.dot(merged_ref[rows, :], w_out_ref[...], preferred_element_type=F32)
    yield
    inv1 = lax.rsqrt(jnp.mean(x1 * x1, axis=-1, keepdims=True) + EPS)
    hn = _bf16(x1 * g_ple_ref[...])
    yield
    ple_gate = _sigmoid(inv1 * jnp.dot(hn, w_pg_ref[...], preferred_element_type=F32))
    x2 = x1 + ple_gate * jnp.dot(_bf16(p_ref[rows, :]), w_pe_ref[...], preferred_element_type=F32)
    yield
    if apply_final_norm:
        x2 = _rms(x2, g_fin_ref[...])
    out_ref[rows, :] = x2


def _tail_kernel(*refs, apply_final_norm):
    n_streams = refs[0].shape[0] // TAIL_ROWS
    live = [_tail_stages(slice(i * TAIL_ROWS, (i + 1) * TAIL_ROWS), *refs, apply_final_norm)
            for i in range(n_streams)]
    while live:
        for g in list(live):
            try:
                next(g)
            except StopIteration:
                live.remove(g)


def _tail(x, merged, p_i, w_out, w_pg, w_pe, g_ple, g_fin, *, apply_final_norm):
    bsz, seq, d = x.shape
    tokens = bsz * seq
    assert tokens % TAIL_TILE == 0
    rows = lambda width: pl.BlockSpec((TAIL_TILE, width), lambda t: (t, 0))
    whole = lambda a: pl.BlockSpec(a.shape, lambda t: (0,) * a.ndim, pipeline_mode=pl.Buffered(1))
    consts = [w_out, w_pg, w_pe, g_ple, g_fin]
    return pl.pallas_call(
        functools.partial(_tail_kernel, apply_final_norm=apply_final_norm),
        grid=(tokens // TAIL_TILE,),
        in_specs=[rows(d), rows(d), rows(PLE_DIM)] + [whole(a) for a in consts],
        out_specs=rows(d),
        out_shape=jax.ShapeDtypeStruct((tokens, d), jnp.float32),
        compiler_params=pltpu.CompilerParams(dimension_semantics=("arbitrary",)),
        name="hybrid_tail",
    )(x.reshape(tokens, d), merged.reshape(tokens, d), p_i.reshape(tokens, PLE_DIM),
      *consts).reshape(bsz, seq, d)


def _const_spec(shape, index=None):
    index = (0,) * len(shape) if index is None else index
    return pl.BlockSpec(shape, lambda b, t: index, pipeline_mode=pl.Buffered(1))


def _layer(layer, x, w_in, w_gk2, big_weights, vectors):
    bsz, seq, d = x.shape
    assert d == D_MODEL and seq % SEQ_TILE == 0 and SEQ_TILE % LANE == 0
    assert w_in.shape[1:] == (D_MODEL, IN_COLS) and COLS_A % GATE_RANK == 0
    assert CONV_K - 1 <= SUBLANE
    nb = 2 if bsz % 2 == 0 else 1
    w_in_t = jnp.swapaxes(w_in, 1, 2)
    tile = pl.BlockSpec((nb, SEQ_TILE, D_MODEL), lambda b, t: (b, t, 0))
    in_hbm = pl.BlockSpec(memory_space=pltpu.HBM)
    in_specs = ([tile] + [in_hbm] * (1 + len(big_weights))
                + [_const_spec((None, GATE_RANK, D_MODEL), (layer, COLS_A // GATE_RANK, 0)),
                   _const_spec((None, GATE_RANK, GLA_KW), (layer, 0, 0))]
                + [_const_spec(vv.shape) if vv.ndim == 2
                   else _const_spec((None,) + vv.shape[1:], (layer, 0, 0)) for vv in vectors])
    bf16 = jnp.bfloat16
    return pl.pallas_call(
        functools.partial(_layer_kernel, layer=layer),
        grid=(bsz // nb, seq // SEQ_TILE),
        in_specs=in_specs,
        out_specs=tile,
        out_shape=jax.ShapeDtypeStruct(x.shape, bf16),
        scratch_shapes=[
            pltpu.VMEM(((COLS_A + COLS_B) // WCHUNK, D_MODEL, WCHUNK), bf16),
            pltpu.VMEM((D_MODEL, LANE), bf16),
            pltpu.VMEM((LANE, GLA_KW), bf16),
            pltpu.VMEM((GLA_VW, D_MODEL), bf16),
            pltpu.VMEM((CONV_WIDTH, D_MODEL), bf16),
            pltpu.VMEM((WSLOTS, WCHUNK, D_MODEL), jnp.float32),
            pltpu.SemaphoreType.DMA((WSLOTS,)),
            pltpu.VMEM((nb, GLA_HEADS, GLA_DV, GLA_DK), jnp.float32),
            pltpu.VMEM((nb, SUBLANE, CONV_WIDTH), jnp.float32),
        ],
        compiler_params=pltpu.CompilerParams(
            dimension_semantics=("arbitrary", "arbitrary"),
            vmem_limit_bytes=VMEM_LIMIT_BYTES),
        name="hybrid_layer",
    )(x, w_in_t, *big_weights, w_in_t, w_gk2, *vectors)


def kernel(x, p, norm_mix_g, w_in, b_merge, w_gk2, b_gk, gla_norm_g, conv_w, w_branch_a,
           w_branch_c, w_out, norm_ple_g, w_ple_gate, w_ple_proj, norm_final_g):
    depth = w_in.shape[0]
    row = lambda a: a.reshape(1, -1).astype(jnp.float32)
    for i in range(depth):
        vectors = [row(norm_mix_g[i]), row(b_gk[i]), row(gla_norm_g[i]),
                   conv_w.astype(jnp.float32), row(b_merge[i])]
        merged = _layer(i, x, w_in, w_gk2, [w_branch_a, w_branch_c], vectors)
        x = _tail(x, merged, p[i], _bf16(w_out[i]), _bf16(w_ple_gate[i]), _bf16(w_ple_proj[i]),
                  row(norm_ple_g[i]), row(norm_final_g), apply_final_norm=(i == depth - 1))
    return x
```

```python
import functools

import jax
import jax.numpy as jnp
from jax import lax
from jax.experimental import pallas as pl
from jax.experimental.pallas import tpu as pltpu

D_MODEL = 1024
GLA_HEADS = 4
GLA_DK = 128
GLA_DV = 256
GLA_KW = GLA_HEADS * GLA_DK
GLA_VW = GLA_HEADS * GLA_DV
GATE_RANK = 16
GATE_NORMALIZER = 16.0
GLA_CHUNK = 64
CONV_WIDTH = 1024
CONV_K = 3
PLE_DIM = 256
EPS = 1e-6

LANE = 128
SUBLANE = 8
SEQ_TILE = 256
WCHUNK = 256
WSLOTS = 4
VMEM_LIMIT_BYTES = 58 * 1024 * 1024

OFF_Q = 0
OFF_K = OFF_Q + GLA_KW
OFF_V = OFF_K + GLA_KW
OFF_ZA = OFF_V + GLA_VW
COLS_A = OFF_ZA + GLA_VW
OFF_CB = 0
OFF_CC = OFF_CB + CONV_WIDTH
OFF_XC = OFF_CC + CONV_WIDTH
OFF_ZC = OFF_XC + CONV_WIDTH
OFF_GA = OFF_ZC + CONV_WIDTH
OFF_GC = OFF_GA + D_MODEL
COLS_B = OFF_GC + D_MODEL
IN_COLS = COLS_A + GATE_RANK + COLS_B

_NT = (((1,), (1,)), ((), ()))
_TN = (((0,), (0,)), ((), ()))
F32 = jnp.float32


def _bf16(a):
    return a.astype(jnp.bfloat16)


def _rms(xf, g):
    ms = jnp.mean(xf * xf, axis=-1, keepdims=True)
    return xf * lax.rsqrt(ms + EPS) * g


def _sigmoid(a):
    return 0.5 * jnp.tanh(0.5 * a) + 0.5


def _log_sigmoid(a):
    return jnp.minimum(a, 0.0) - jnp.log(1.0 + jnp.exp(-jnp.abs(a)))


def _round_up(n, m):
    return (n + m - 1) // m * m


def _load_weights(layer, w_in_t_hbm, dense_hbm, gate_rows_ref, w_gk2_f32_ref,
                  w_in_ref, w_gkl_ref, w_gk2_ref, dense_refs, stage_ref, sem_ref):
    ahead = WSLOTS - 1
    n_in = (COLS_A + COLS_B) // WCHUNK
    dense = [(hbm, dst, r) for hbm, dst in zip(dense_hbm, dense_refs)
             for r in range(0, dst.shape[0], WCHUNK)]
    assert ahead <= len(dense) and ahead < n_in

    def copy(src, g):
        return pltpu.make_async_copy(src, stage_ref.at[g % WSLOTS], sem_ref.at[g % WSLOTS])

    def in_proj_copy(c):
        first = c * WCHUNK + jnp.where(c >= COLS_A // WCHUNK, GATE_RANK, 0)
        return copy(w_in_t_hbm.at[layer, pl.ds(first, WCHUNK), :], c)

    def dense_copy(i):
        hbm, _, r = dense[i]
        return copy(hbm.at[layer, pl.ds(r, WCHUNK), :], n_in + i)

    def in_proj_consume(c):
        w_in_ref[c] = _bf16(stage_ref[c % WSLOTS].T)

    for c in range(ahead):
        in_proj_copy(c).start(priority=c % 2)

    unroll = 3
    assert (n_in - ahead) % unroll == 0

    def body(i, carry):
        for j in range(unroll):
            c = i * unroll + j
            in_proj_copy(c).wait()
            in_proj_copy(c + ahead).start(priority=j % 2)
            in_proj_consume(c)
        return carry

    lax.fori_loop(0, (n_in - ahead) // unroll, body, 0)
    for c in range(n_in - ahead, n_in):
        in_proj_copy(c).wait()
        dense_copy(c + ahead - n_in).start(priority=c % 2)
        in_proj_consume(c)
    for i, (_, dst, r) in enumerate(dense):
        dense_copy(i).wait()
        if i + ahead < len(dense):
            dense_copy(i + ahead).start(priority=i % 2)
        dst[r:r + WCHUNK, :] = _bf16(stage_ref[(n_in + i) % WSLOTS])

    gate_rows = jnp.concatenate(
        [gate_rows_ref[...], jnp.zeros((LANE - GATE_RANK, D_MODEL), F32)], axis=0)
    w_gkl_ref[...] = _bf16(gate_rows.T)
    w_gk2_ref[...] = jnp.zeros_like(w_gk2_ref)
    w_gk2_ref[:GATE_RANK, :] = _bf16(w_gk2_f32_ref[...])


def _tile_stages(s, x_ref, p_ref, w_in_ref, w_gkl_ref, w_gk2_ref, w_a_ref,
                 w_c_ref, w_out_ref, w_pg_ref, w_pe_ref, g_mix_ref, b_gk_ref, g_gla_ref,
                 conv_w_ref, b_merge_ref, g_ple_ref, g_fin_ref, out_ref, state_ref, carry_ref,
                 apply_final_norm):
    ts = x_ref.shape[1]
    n_chunks = ts // GLA_CHUNK

    x = x_ref[s]
    inv = jnp.broadcast_to(
        lax.rsqrt(jnp.mean(x * x, axis=-1, keepdims=True) + EPS), (ts, D_MODEL))
    h = _bf16(x * g_mix_ref[...])
    yield

    def proj(first_chunk, off, width):
        return inv[:, :width] * jnp.concatenate(
            [jnp.dot(h, w_in_ref[first_chunk + c], preferred_element_type=F32)
             for c in range(off // WCHUNK, (off + width) // WCHUNK)], axis=1)

    proj_a = functools.partial(proj, 0)
    proj_b = functools.partial(proj, COLS_A // WCHUNK)

    gk_low = _bf16(inv[:, :LANE] * jnp.dot(h, w_gkl_ref[...], preferred_element_type=F32))
    q = proj_a(OFF_Q, GLA_KW)
    k = proj_a(OFF_K, GLA_KW)
    yield
    pre = jnp.dot(gk_low, w_gk2_ref[...], preferred_element_type=F32) + b_gk_ref[...]
    v = _bf16(proj_a(OFF_V, GLA_VW))
    yield
    z_a = proj_a(OFF_ZA, GLA_VW)
    gk = _log_sigmoid(pre) * (1.0 / GATE_NORMALIZER)
    act_a = _bf16(z_a * _sigmoid(z_a))
    yield
    m = proj_b(OFF_CC, CONV_WIDTH) * proj_b(OFF_XC, CONV_WIDTH)
    cum = gk
    shift = 1
    while shift < ts:
        cum = cum + jnp.concatenate(
            [jnp.zeros((shift, GLA_KW), F32), cum[:ts - shift, :]], axis=0)
        shift *= 2
    row = lax.broadcasted_iota(jnp.int32, (ts, ts), 0)
    col = lax.broadcasted_iota(jnp.int32, (ts, ts), 1)
    causal = col <= row
    yield
    cb = proj_b(OFF_CB, CONV_WIDTH)
    cum_end = cum[ts - 1:ts, :]
    starts = [jnp.zeros((1, GLA_KW), F32)]
    starts += [cum[c * GLA_CHUNK - 1:c * GLA_CHUNK, :] for c in range(1, n_chunks)]
    start_full = jnp.concatenate(
        [jnp.broadcast_to(st, (GLA_CHUNK, GLA_KW)) for st in starts], axis=0)
    exp_start_full = jnp.concatenate(
        [jnp.broadcast_to(jnp.exp(st), (GLA_CHUNK, GLA_KW)) for st in starts], axis=0)
    q_loc = q * (jnp.exp(cum - start_full) * (GLA_DK ** -0.5))
    q_in = _bf16(q_loc)
    q_tile = _bf16(q_loc * exp_start_full)
    k_end = _bf16(k * jnp.exp(cum_end - cum))
    k_rel = []
    for c in range(n_chunks):
        n = _round_up((c + 1) * GLA_CHUNK, LANE)
        k_rel.append(_bf16(k[:n, :] * jnp.exp(starts[c] - cum[:n, :])))
    tile_decay = jnp.exp(cum_end)
    ext = jnp.concatenate([carry_ref[s], m], axis=0)
    carry_ref[s] = m[ts - SUBLANE:, :]
    taps = [m] + [pltpu.roll(ext, j, 0)[SUBLANE:, :] for j in range(1, CONV_K)]
    cw = conv_w_ref[...]
    conv = cw[0:1, :] * taps[CONV_K - 1]
    for j in range(1, CONV_K):
        conv = conv + cw[j:j + 1, :] * taps[CONV_K - 1 - j]
    u = cb * conv
    yield

    attn, s_old = [], []
    for hd in range(GLA_HEADS):
        ks = slice(hd * GLA_DK, (hd + 1) * GLA_DK)
        vs = slice(hd * GLA_DV, (hd + 1) * GLA_DV)
        s_t = state_ref[s, hd]
        s_old.append(_bf16(s_t))
        blocks = []
        for c in range(n_chunks):
            rs = slice(c * GLA_CHUNK, (c + 1) * GLA_CHUNK)
            a_c = lax.dot_general(q_in[rs, ks], k_rel[c][:, ks], _NT,
                                  preferred_element_type=F32)
            if a_c.shape[1] < ts:
                a_c = jnp.concatenate(
                    [a_c, jnp.zeros((GLA_CHUNK, ts - a_c.shape[1]), F32)], axis=1)
            blocks.append(a_c)
        attn.append(_bf16(jnp.where(causal, jnp.concatenate(blocks, axis=0), 0.0)))
        state_ref[s, hd] = s_t * tile_decay[:, ks] + lax.dot_general(
            v[:, vs], k_end[:, ks], _TN, preferred_element_type=F32)
    z_c = proj_b(OFF_ZC, CONV_WIDTH)
    gated_c = _bf16(u * (z_c * _sigmoid(z_c)))
    yield
    g_gla = g_gla_ref[...]
    gated = []
    for hd in range(GLA_HEADS):
        ks = slice(hd * GLA_DK, (hd + 1) * GLA_DK)
        vs = slice(hd * GLA_DV, (hd + 1) * GLA_DV)
        o = (lax.dot_general(q_tile[:, ks], s_old[hd], _NT, preferred_element_type=F32)
             + jnp.dot(attn[hd], v[:, vs], preferred_element_type=F32))
        gated.append(_bf16(_rms(o, g_gla) * act_a[:, vs]))
    b_merge = b_merge_ref[...]
    gate_c = _bf16(_sigmoid(proj_b(OFF_GC, D_MODEL) + b_merge[:, D_MODEL:]))
    yield
    y_c = _bf16(gate_c * jnp.dot(gated_c, w_c_ref[...], preferred_element_type=F32))
    gate_a = _bf16(_sigmoid(proj_b(OFF_GA, D_MODEL) + b_merge[:, :D_MODEL]))
    yield
    y_a = jnp.dot(jnp.concatenate(gated, axis=1), w_a_ref[...], preferred_element_type=F32)
    merged = _bf16(gate_a * y_a + y_c)
    yield

    x1 = x + jnp.dot(merged, w_out_ref[...], preferred_element_type=F32)
    yield
    inv1 = lax.rsqrt(jnp.mean(x1 * x1, axis=-1, keepdims=True) + EPS)
    hn = _bf16(x1 * g_ple_ref[...])
    yield
    ple_gate = _sigmoid(inv1 * jnp.dot(hn, w_pg_ref[...], preferred_element_type=F32))
    x2 = x1 + ple_gate * jnp.dot(_bf16(p_ref[s]), w_pe_ref[...], preferred_element_type=F32)
    yield
    if apply_final_norm:
        x2 = _rms(x2, g_fin_ref[...])
    out_ref[s] = x2


def _layer_kernel(x_ref, p_ref, w_in_t_hbm, wa_hbm, wc_hbm, wo_hbm, wpg_hbm, wpe_hbm,
                  gate_rows_ref, w_gk2_f32_ref, g_mix_ref, b_gk_ref, g_gla_ref, conv_w_ref,
                  b_merge_ref, g_ple_ref, g_fin_ref, out_ref,
                  w_in_ref, w_gkl_ref, w_gk2_ref, w_a_ref, w_c_ref, w_out_ref,
                  w_pg_ref, w_pe_ref, stage_ref, sem_ref, state_ref, carry_ref,
                  *, layer, apply_final_norm):
    @pl.when((pl.program_id(0) == 0) & (pl.program_id(1) == 0))
    def _():
        _load_weights(layer, w_in_t_hbm, (wa_hbm, wc_hbm, wo_hbm, wpg_hbm, wpe_hbm),
                      gate_rows_ref, w_gk2_f32_ref, w_in_ref, w_gkl_ref, w_gk2_ref,
                      (w_a_ref, w_c_ref, w_out_ref, w_pg_ref, w_pe_ref), stage_ref, sem_ref)

    @pl.when(pl.program_id(1) == 0)
    def _():
        state_ref[...] = jnp.zeros_like(state_ref)
        carry_ref[...] = jnp.zeros_like(carry_ref)

    live = [_tile_stages(s, x_ref, p_ref, w_in_ref, w_gkl_ref, w_gk2_ref, w_a_ref,
                         w_c_ref, w_out_ref, w_pg_ref, w_pe_ref, g_mix_ref, b_gk_ref, g_gla_ref,
                         conv_w_ref, b_merge_ref, g_ple_ref, g_fin_ref, out_ref, state_ref,
                         carry_ref, apply_final_norm)
            for s in range(x_ref.shape[0])]
    while live:
        for g in list(live):
            try:
                next(g)
            except StopIteration:
                live.remove(g)


def _const_spec(shape, index=None):
    index = (0,) * len(shape) if index is None else index
    return pl.BlockSpec(shape, lambda b, t: index, pipeline_mode=pl.Buffered(1))


def _layer(layer, x, p_i, w_in, w_gk2, big_weights, vectors, *, apply_final_norm):
    bsz, seq, d = x.shape
    assert d == D_MODEL and seq % SEQ_TILE == 0 and SEQ_TILE % LANE == 0
    assert w_in.shape[1:] == (D_MODEL, IN_COLS) and COLS_A % GATE_RANK == 0
    assert CONV_K - 1 <= SUBLANE
    nb = 2 if bsz % 2 == 0 else 1
    w_in_t = jnp.swapaxes(w_in, 1, 2)
    tile = lambda width: pl.BlockSpec((nb, SEQ_TILE, width), lambda b, t: (b, t, 0))
    in_hbm = pl.BlockSpec(memory_space=pltpu.HBM)
    in_specs = ([tile(D_MODEL), tile(PLE_DIM)] + [in_hbm] * (1 + len(big_weights))
                + [_const_spec((None, GATE_RANK, D_MODEL), (layer, COLS_A // GATE_RANK, 0)),
                   _const_spec((None, GATE_RANK, GLA_KW), (layer, 0, 0))]
                + [_const_spec(vv.shape) if vv.ndim == 2
                   else _const_spec((None,) + vv.shape[1:], (layer, 0, 0)) for vv in vectors])
    bf16 = jnp.bfloat16
    return pl.pallas_call(
        functools.partial(_layer_kernel, layer=layer, apply_final_norm=apply_final_norm),
        grid=(bsz // nb, seq // SEQ_TILE),
        in_specs=in_specs,
        out_specs=tile(D_MODEL),
        out_shape=jax.ShapeDtypeStruct(x.shape, jnp.float32),
        scratch_shapes=[
            pltpu.VMEM(((COLS_A + COLS_B) // WCHUNK, D_MODEL, WCHUNK), bf16),
            pltpu.VMEM((D_MODEL, LANE), bf16),
            pltpu.VMEM((LANE, GLA_KW), bf16),
            pltpu.VMEM((GLA_VW, D_MODEL), bf16),
            pltpu.VMEM((CONV_WIDTH, D_MODEL), bf16),
            pltpu.VMEM((D_MODEL, D_MODEL), bf16),
            pltpu.VMEM((D_MODEL, D_MODEL), bf16),
            pltpu.VMEM((PLE_DIM, D_MODEL), bf16),
            pltpu.VMEM((WSLOTS, WCHUNK, D_MODEL), jnp.float32),
            pltpu.SemaphoreType.DMA((WSLOTS,)),
            pltpu.VMEM((nb, GLA_HEADS, GLA_DV, GLA_DK), jnp.float32),
            pltpu.VMEM((nb, SUBLANE, CONV_WIDTH), jnp.float32),
        ],
        compiler_params=pltpu.CompilerParams(
            dimension_semantics=("arbitrary", "arbitrary"),
            vmem_limit_bytes=VMEM_LIMIT_BYTES),
        name="hybrid_layer",
    )(x, p_i, w_in_t, *big_weights, w_in_t, w_gk2, *vectors)


def kernel(x, p, norm_mix_g, w_in, b_merge, w_gk2, b_gk, gla_norm_g, conv_w, w_branch_a,
           w_branch_c, w_out, norm_ple_g, w_ple_gate, w_ple_proj, norm_final_g):
    depth = w_in.shape[0]
    row = lambda a: a.reshape(1, -1).astype(jnp.float32)
    for i in range(depth):
        vectors = [row(norm_mix_g[i]), row(b_gk[i]), row(gla_norm_g[i]),
                   conv_w.astype(jnp.float32), row(b_merge[i]), row(norm_ple_g[i]),
                   row(norm_final_g)]
        x = _layer(i, x, p[i], w_in, w_gk2,
                   [w_branch_a, w_branch_c, w_out, w_ple_gate, w_ple_proj], vectors,
                   apply_final_norm=(i == depth - 1))
    return x
```

```python
import functools

import jax
import jax.numpy as jnp
from jax import lax
from jax.experimental import pallas as pl
from jax.experimental.pallas import tpu as pltpu

D_MODEL = 1024
GLA_HEADS = 4
GLA_DK = 128
GLA_DV = 256
GLA_KW = GLA_HEADS * GLA_DK
GLA_VW = GLA_HEADS * GLA_DV
GATE_RANK = 16
GATE_NORMALIZER = 16.0
GLA_CHUNK = 64
CONV_WIDTH = 1024
CONV_K = 3
PLE_DIM = 256
EPS = 1e-6

LANE = 128
SUBLANE = 8
SEQ_TILE = 256
WCHUNK = 256
WSLOTS = 4
VMEM_LIMIT_BYTES = 58 * 1024 * 1024

ROW_G_MIX, ROW_G_PLE, ROW_G_FIN, ROW_B_GK, ROW_G_GLA, ROW_B_MERGE, ROW_CONV = 0, 1, 2, 3, 4, 5, 7
VEC_ROWS = 16

OFF_Q = 0
OFF_K = OFF_Q + GLA_KW
OFF_V = OFF_K + GLA_KW
OFF_ZA = OFF_V + GLA_VW
COLS_A = OFF_ZA + GLA_VW
OFF_CB = 0
OFF_CC = OFF_CB + CONV_WIDTH
OFF_XC = OFF_CC + CONV_WIDTH
OFF_ZC = OFF_XC + CONV_WIDTH
OFF_GA = OFF_ZC + CONV_WIDTH
OFF_GC = OFF_GA + D_MODEL
COLS_B = OFF_GC + D_MODEL
IN_COLS = COLS_A + GATE_RANK + COLS_B

_NT = (((1,), (1,)), ((), ()))
_TN = (((0,), (0,)), ((), ()))
F32 = jnp.float32


def _bf16(a):
    return a.astype(jnp.bfloat16)


def _rms(xf, g):
    ms = jnp.mean(xf * xf, axis=-1, keepdims=True)
    return xf * lax.rsqrt(ms + EPS) * g


def _sigmoid(a):
    return 0.5 * jnp.tanh(0.5 * a) + 0.5


def _log_sigmoid(a):
    return jnp.minimum(a, 0.0) - jnp.log(1.0 + jnp.exp(-jnp.abs(a)))


def _round_up(n, m):
    return (n + m - 1) // m * m


def _load_weights(layer, w_in_t_hbm, dense_hbm, gate_rows_ref, w_gk2_f32_ref,
                  w_in_ref, w_gkl_ref, w_gk2_ref, dense_refs, stage_ref, sem_ref):
    ahead = WSLOTS - 1
    n_in = (COLS_A + COLS_B) // WCHUNK
    dense = [(hbm, dst, r) for hbm, dst in zip(dense_hbm, dense_refs)
             for r in range(0, dst.shape[0], WCHUNK)]
    assert ahead <= len(dense) and ahead < n_in

    def copy(src, g):
        return pltpu.make_async_copy(src, stage_ref.at[g % WSLOTS], sem_ref.at[g % WSLOTS])

    def in_proj_copy(c):
        first = c * WCHUNK + jnp.where(c >= COLS_A // WCHUNK, GATE_RANK, 0)
        return copy(w_in_t_hbm.at[layer, pl.ds(first, WCHUNK), :], c)

    def dense_copy(i):
        hbm, _, r = dense[i]
        return copy(hbm.at[layer, pl.ds(r, WCHUNK), :], n_in + i)

    def in_proj_consume(c):
        w_in_ref[c] = _bf16(stage_ref[c % WSLOTS].T)

    for c in range(ahead):
        in_proj_copy(c).start(priority=c % 2)

    unroll = 3
    assert (n_in - ahead) % unroll == 0

    def body(i, carry):
        for j in range(unroll):
            c = i * unroll + j
            in_proj_copy(c).wait()
            in_proj_copy(c + ahead).start(priority=j % 2)
            in_proj_consume(c)
        return carry

    lax.fori_loop(0, (n_in - ahead) // unroll, body, 0)
    for c in range(n_in - ahead, n_in):
        in_proj_copy(c).wait()
        dense_copy(c + ahead - n_in).start(priority=c % 2)
        in_proj_consume(c)
    for i, (_, dst, r) in enumerate(dense):
        dense_copy(i).wait()
        if i + ahead < len(dense):
            dense_copy(i + ahead).start(priority=i % 2)
        dst[r:r + WCHUNK, :] = _bf16(stage_ref[(n_in + i) % WSLOTS])

    gate_rows = jnp.concatenate(
        [gate_rows_ref[...], jnp.zeros((LANE - GATE_RANK, D_MODEL), F32)], axis=0)
    w_gkl_ref[...] = _bf16(gate_rows.T)
    w_gk2_ref[...] = jnp.zeros_like(w_gk2_ref)
    w_gk2_ref[:GATE_RANK, :] = _bf16(w_gk2_f32_ref[...])


def _tile_stages(s, x_ref, p_ref, w_in_ref, w_gkl_ref, w_gk2_ref, w_a_ref,
                 w_c_ref, w_out_ref, w_pg_ref, w_pe_ref, vec_ref, out_ref, state_ref, carry_ref,
                 apply_final_norm):
    ts = x_ref.shape[1]
    n_chunks = ts // GLA_CHUNK
    g_mix, g_ple, g_fin = (vec_ref[r:r + 1, :] for r in (ROW_G_MIX, ROW_G_PLE, ROW_G_FIN))

    x = x_ref[s]
    inv = jnp.broadcast_to(
        lax.rsqrt(jnp.mean(x * x, axis=-1, keepdims=True) + EPS), (ts, D_MODEL))
    h = _bf16(x * g_mix)
    yield

    def proj(first_chunk, off, width):
        return inv[:, :width] * jnp.concatenate(
            [jnp.dot(h, w_in_ref[first_chunk + c], preferred_element_type=F32)
             for c in range(off // WCHUNK, (off + width) // WCHUNK)], axis=1)

    proj_a = functools.partial(proj, 0)
    proj_b = functools.partial(proj, COLS_A // WCHUNK)

    gk_low = _bf16(inv[:, :LANE] * jnp.dot(h, w_gkl_ref[...], preferred_element_type=F32))
    q = proj_a(OFF_Q, GLA_KW)
    k = proj_a(OFF_K, GLA_KW)
    yield
    pre = jnp.dot(gk_low, w_gk2_ref[...], preferred_element_type=F32) + vec_ref[ROW_B_GK:ROW_B_GK + 1, :GLA_KW]
    v = _bf16(proj_a(OFF_V, GLA_VW))
    yield
    z_a = proj_a(OFF_ZA, GLA_VW)
    gk = _log_sigmoid(pre) * (1.0 / GATE_NORMALIZER)
    act_a = _bf16(z_a * _sigmoid(z_a))
    yield
    m = proj_b(OFF_CC, CONV_WIDTH) * proj_b(OFF_XC, CONV_WIDTH)
    cum = gk
    shift = 1
    while shift < ts:
        cum = cum + jnp.concatenate(
            [jnp.zeros((shift, GLA_KW), F32), cum[:ts - shift, :]], axis=0)
        shift *= 2
    row = lax.broadcasted_iota(jnp.int32, (ts, ts), 0)
    col = lax.broadcasted_iota(jnp.int32, (ts, ts), 1)
    causal = col <= row
    yield
    cb = proj_b(OFF_CB, CONV_WIDTH)
    cum_end = cum[ts - 1:ts, :]
    starts = [jnp.zeros((1, GLA_KW), F32)]
    starts += [cum[c * GLA_CHUNK - 1:c * GLA_CHUNK, :] for c in range(1, n_chunks)]
    start_full = jnp.concatenate(
        [jnp.broadcast_to(st, (GLA_CHUNK, GLA_KW)) for st in starts], axis=0)
    exp_start_full = jnp.concatenate(
        [jnp.broadcast_to(jnp.exp(st), (GLA_CHUNK, GLA_KW)) for st in starts], axis=0)
    q_loc = q * (jnp.exp(cum - start_full) * (GLA_DK ** -0.5))
    q_in = _bf16(q_loc)
    q_tile = _bf16(q_loc * exp_start_full)
    k_end = _bf16(k * jnp.exp(cum_end - cum))
    k_rel = []
    for c in range(n_chunks):
        n = _round_up((c + 1) * GLA_CHUNK, LANE)
        k_rel.append(_bf16(k[:n, :] * jnp.exp(starts[c] - cum[:n, :])))
    tile_decay = jnp.exp(cum_end)
    ext = jnp.concatenate([carry_ref[s], m], axis=0)
    carry_ref[s] = m[ts - SUBLANE:, :]
    taps = [m] + [pltpu.roll(ext, j, 0)[SUBLANE:, :] for j in range(1, CONV_K)]
    cw = vec_ref[ROW_CONV:ROW_CONV + CONV_K, :]
    conv = cw[0:1, :] * taps[CONV_K - 1]
    for j in range(1, CONV_K):
        conv = conv + cw[j:j + 1, :] * taps[CONV_K - 1 - j]
    u = cb * conv
    yield

    attn, s_old = [], []
    for hd in range(GLA_HEADS):
        ks = slice(hd * GLA_DK, (hd + 1) * GLA_DK)
        vs = slice(hd * GLA_DV, (hd + 1) * GLA_DV)
        s_t = state_ref[s, hd]
        s_old.append(_bf16(s_t))
        blocks = []
        for c in range(n_chunks):
            rs = slice(c * GLA_CHUNK, (c + 1) * GLA_CHUNK)
            a_c = lax.dot_general(q_in[rs, ks], k_rel[c][:, ks], _NT,
                                  preferred_element_type=F32)
            if a_c.shape[1] < ts:
                a_c = jnp.concatenate(
                    [a_c, jnp.zeros((GLA_CHUNK, ts - a_c.shape[1]), F32)], axis=1)
            blocks.append(a_c)
        attn.append(_bf16(jnp.where(causal, jnp.concatenate(blocks, axis=0), 0.0)))
        state_ref[s, hd] = s_t * tile_decay[:, ks] + lax.dot_general(
            v[:, vs], k_end[:, ks], _TN, preferred_element_type=F32)
    z_c = proj_b(OFF_ZC, CONV_WIDTH)
    gated_c = _bf16(u * (z_c * _sigmoid(z_c)))
    yield
    g_gla = vec_ref[ROW_G_GLA:ROW_G_GLA + 1, :GLA_DV]
    gated = []
    for hd in range(GLA_HEADS):
        ks = slice(hd * GLA_DK, (hd + 1) * GLA_DK)
        vs = slice(hd * GLA_DV, (hd + 1) * GLA_DV)
        o = (lax.dot_general(q_tile[:, ks], s_old[hd], _NT, preferred_element_type=F32)
             + jnp.dot(attn[hd], v[:, vs], preferred_element_type=F32))
        gated.append(_bf16(_rms(o, g_gla) * act_a[:, vs]))
    gate_c = _bf16(_sigmoid(proj_b(OFF_GC, D_MODEL) + vec_ref[ROW_B_MERGE + 1:ROW_B_MERGE + 2, :]))
    yield
    y_c = _bf16(gate_c * jnp.dot(gated_c, w_c_ref[...], preferred_element_type=F32))
    gate_a = _bf16(_sigmoid(proj_b(OFF_GA, D_MODEL) + vec_ref[ROW_B_MERGE:ROW_B_MERGE + 1, :]))
    yield
    y_a = jnp.dot(jnp.concatenate(gated, axis=1), w_a_ref[...], preferred_element_type=F32)
    merged = _bf16(gate_a * y_a + y_c)
    yield

    x1 = x + jnp.dot(merged, w_out_ref[...], preferred_element_type=F32)
    yield
    inv1 = lax.rsqrt(jnp.mean(x1 * x1, axis=-1, keepdims=True) + EPS)
    hn = _bf16(x1 * g_ple)
    yield
    ple_gate = _sigmoid(inv1 * jnp.dot(hn, w_pg_ref[...], preferred_element_type=F32))
    x2 = x1 + ple_gate * jnp.dot(_bf16(p_ref[s]), w_pe_ref[...], preferred_element_type=F32)
    yield
    if apply_final_norm:
        x2 = _rms(x2, g_fin)
    out_ref[s] = x2


def _layer_kernel(x_ref, p_ref, w_in_t_hbm, wa_hbm, wc_hbm, wo_hbm, wpg_hbm, wpe_hbm,
                  gate_rows_ref, w_gk2_f32_ref, vec_ref, out_ref,
                  w_in_ref, w_gkl_ref, w_gk2_ref, w_a_ref, w_c_ref, w_out_ref,
                  w_pg_ref, w_pe_ref, stage_ref, sem_ref, state_ref, carry_ref,
                  *, layer, apply_final_norm):
    @pl.when((pl.program_id(0) == 0) & (pl.program_id(1) == 0))
    def _():
        _load_weights(layer, w_in_t_hbm, (wa_hbm, wc_hbm, wo_hbm, wpg_hbm, wpe_hbm),
                      gate_rows_ref, w_gk2_f32_ref, w_in_ref, w_gkl_ref, w_gk2_ref,
                      (w_a_ref, w_c_ref, w_out_ref, w_pg_ref, w_pe_ref), stage_ref, sem_ref)

    @pl.when(pl.program_id(1) == 0)
    def _():
        state_ref[...] = jnp.zeros_like(state_ref)
        carry_ref[...] = jnp.zeros_like(carry_ref)

    live = [_tile_stages(s, x_ref, p_ref, w_in_ref, w_gkl_ref, w_gk2_ref, w_a_ref,
                         w_c_ref, w_out_ref, w_pg_ref, w_pe_ref, vec_ref, out_ref, state_ref,
                         carry_ref, apply_final_norm)
            for s in range(x_ref.shape[0])]
    while live:
        for g in list(live):
            try:
                next(g)
            except StopIteration:
                live.remove(g)


def _const_spec(shape, index=None):
    index = (0,) * len(shape) if index is None else index
    return pl.BlockSpec(shape, lambda b, t: index, pipeline_mode=pl.Buffered(1))


def _layer(layer, x, p_i, w_in, w_gk2, big_weights, packed, *, apply_final_norm):
    bsz, seq, d = x.shape
    assert d == D_MODEL and seq % SEQ_TILE == 0 and SEQ_TILE % LANE == 0
    assert w_in.shape[1:] == (D_MODEL, IN_COLS) and COLS_A % GATE_RANK == 0
    assert CONV_K - 1 <= SUBLANE
    nb = 2 if bsz % 2 == 0 else 1
    w_in_t = jnp.swapaxes(w_in, 1, 2)
    tile = lambda width: pl.BlockSpec((nb, SEQ_TILE, width), lambda b, t: (b, t, 0))
    in_hbm = pl.BlockSpec(memory_space=pltpu.HBM)
    in_specs = ([tile(D_MODEL), tile(PLE_DIM)] + [in_hbm] * (1 + len(big_weights))
                + [_const_spec((None, GATE_RANK, D_MODEL), (layer, COLS_A // GATE_RANK, 0)),
                   _const_spec((None, GATE_RANK, GLA_KW), (layer, 0, 0))]
                + [_const_spec(packed.shape)])
    bf16 = jnp.bfloat16
    return pl.pallas_call(
        functools.partial(_layer_kernel, layer=layer, apply_final_norm=apply_final_norm),
        grid=(bsz // nb, seq // SEQ_TILE),
        in_specs=in_specs,
        out_specs=tile(D_MODEL),
        out_shape=jax.ShapeDtypeStruct(x.shape, jnp.float32),
        scratch_shapes=[
            pltpu.VMEM(((COLS_A + COLS_B) // WCHUNK, D_MODEL, WCHUNK), bf16),
            pltpu.VMEM((D_MODEL, LANE), bf16),
            pltpu.VMEM((LANE, GLA_KW), bf16),
            pltpu.VMEM((GLA_VW, D_MODEL), bf16),
            pltpu.VMEM((CONV_WIDTH, D_MODEL), bf16),
            pltpu.VMEM((D_MODEL, D_MODEL), bf16),
            pltpu.VMEM((D_MODEL, D_MODEL), bf16),
            pltpu.VMEM((PLE_DIM, D_MODEL), bf16),
            pltpu.VMEM((WSLOTS, WCHUNK, D_MODEL), jnp.float32),
            pltpu.SemaphoreType.DMA((WSLOTS,)),
            pltpu.VMEM((nb, GLA_HEADS, GLA_DV, GLA_DK), jnp.float32),
            pltpu.VMEM((nb, SUBLANE, CONV_WIDTH), jnp.float32),
        ],
        compiler_params=pltpu.CompilerParams(
            dimension_semantics=("arbitrary", "arbitrary"),
            vmem_limit_bytes=VMEM_LIMIT_BYTES),
        name="hybrid_layer",
    )(x, p_i, w_in_t, *big_weights, w_in_t, w_gk2, packed)


def _packed_vectors(i, norm_mix_g, norm_ple_g, norm_final_g, b_gk, gla_norm_g, b_merge, conv_w):
    wide = lambda a: jnp.pad(a.astype(F32), (0, D_MODEL - a.shape[0]))[None, :]
    rows = [wide(norm_mix_g[i]), wide(norm_ple_g[i]), wide(norm_final_g), wide(b_gk[i]),
            wide(gla_norm_g[i]), b_merge[i].astype(F32).reshape(2, D_MODEL), conv_w[i].astype(F32)]
    rows.append(jnp.zeros((VEC_ROWS - ROW_CONV - CONV_K, D_MODEL), F32))
    return jnp.concatenate(rows, axis=0)


def kernel(x, p, norm_mix_g, w_in, b_merge, w_gk2, b_gk, gla_norm_g, conv_w, w_branch_a,
           w_branch_c, w_out, norm_ple_g, w_ple_gate, w_ple_proj, norm_final_g):
    depth = w_in.shape[0]
    for i in range(depth):
        packed = _packed_vectors(i, norm_mix_g, norm_ple_g, norm_final_g, b_gk, gla_norm_g,
                                 b_merge, conv_w)
        x = _layer(i, x, p[i], w_in, w_gk2,
                   [w_branch_a, w_branch_c, w_out, w_ple_gate, w_ple_proj], packed,
                   apply_final_norm=(i == depth - 1))
    return x
```

```python
import functools

import jax
import jax.numpy as jnp
from jax import lax
from jax.experimental import pallas as pl
from jax.experimental.pallas import tpu as pltpu

D_MODEL = 1024
GLA_HEADS = 4
GLA_DK = 128
GLA_DV = 256
GLA_KW = GLA_HEADS * GLA_DK
GLA_VW = GLA_HEADS * GLA_DV
GATE_RANK = 16
GATE_NORMALIZER = 16.0
GLA_CHUNK = 64
CONV_WIDTH = 1024
CONV_K = 3
PLE_DIM = 256
EPS = 1e-6

LANE = 128
SUBLANE = 8
SEQ_TILE = 256
WCHUNK = 256
WSLOTS = 4
VMEM_LIMIT_BYTES = 58 * 1024 * 1024

OFF_Q = 0
OFF_K = OFF_Q + GLA_KW
OFF_V = OFF_K + GLA_KW
OFF_ZA = OFF_V + GLA_VW
COLS_A = OFF_ZA + GLA_VW
OFF_CB = 0
OFF_CC = OFF_CB + CONV_WIDTH
OFF_XC = OFF_CC + CONV_WIDTH
OFF_ZC = OFF_XC + CONV_WIDTH
OFF_GA = OFF_ZC + CONV_WIDTH
OFF_GC = OFF_GA + D_MODEL
COLS_B = OFF_GC + D_MODEL
IN_COLS = COLS_A + GATE_RANK + COLS_B

_NT = (((1,), (1,)), ((), ()))
_TN = (((0,), (0,)), ((), ()))
F32 = jnp.float32


def _bf16(a):
    return a.astype(jnp.bfloat16)


def _rms(xf, g):
    ms = jnp.mean(xf * xf, axis=-1, keepdims=True)
    return xf * lax.rsqrt(ms + EPS) * g


def _sigmoid(a):
    return 0.5 * jnp.tanh(0.5 * a) + 0.5


def _log_sigmoid(a):
    return jnp.minimum(a, 0.0) - jnp.log(1.0 + jnp.exp(-jnp.abs(a)))


def _round_up(n, m):
    return (n + m - 1) // m * m


def _load_weights(layer, w_in_t_hbm, dense_hbm, gate_rows_ref, w_gk2_f32_ref,
                  w_in_ref, w_gkl_ref, w_gk2_ref, dense_refs, stage_ref, sem_ref):
    ahead = WSLOTS - 1
    n_in = (COLS_A + COLS_B) // WCHUNK
    dense = [(hbm, dst, r) for hbm, dst in zip(dense_hbm, dense_refs)
             for r in range(0, dst.shape[0], WCHUNK)]
    assert ahead <= len(dense) and ahead < n_in

    def copy(src, g):
        return pltpu.make_async_copy(src, stage_ref.at[g % WSLOTS], sem_ref.at[g % WSLOTS])

    def in_proj_copy(c):
        first = c * WCHUNK + jnp.where(c >= COLS_A // WCHUNK, GATE_RANK, 0)
        return copy(w_in_t_hbm.at[layer, pl.ds(first, WCHUNK), :], c)

    def dense_copy(i):
        hbm, _, r = dense[i]
        return copy(hbm.at[layer, pl.ds(r, WCHUNK), :], n_in + i)

    def in_proj_consume(c):
        w_in_ref[c] = _bf16(stage_ref[c % WSLOTS].T)

    for c in range(ahead):
        in_proj_copy(c).start(priority=c % 2)

    unroll = 3
    assert (n_in - ahead) % unroll == 0

    def body(i, carry):
        for j in range(unroll):
            c = i * unroll + j
            in_proj_copy(c).wait()
            in_proj_copy(c + ahead).start(priority=j % 2)
            in_proj_consume(c)
        return carry

    lax.fori_loop(0, (n_in - ahead) // unroll, body, 0)
    for c in range(n_in - ahead, n_in):
        in_proj_copy(c).wait()
        dense_copy(c + ahead - n_in).start(priority=c % 2)
        in_proj_consume(c)
    for i, (_, dst, r) in enumerate(dense):
        dense_copy(i).wait()
        if i + ahead < len(dense):
            dense_copy(i + ahead).start(priority=i % 2)
        dst[r:r + WCHUNK, :] = _bf16(stage_ref[(n_in + i) % WSLOTS])

    gate_rows = jnp.concatenate(
        [gate_rows_ref[...], jnp.zeros((LANE - GATE_RANK, D_MODEL), F32)], axis=0)
    w_gkl_ref[...] = _bf16(gate_rows.T)
    w_gk2_ref[...] = jnp.zeros_like(w_gk2_ref)
    w_gk2_ref[:GATE_RANK, :] = _bf16(w_gk2_f32_ref[...])


def _tile_stages(s, x_ref, p_ref, w_in_ref, w_gkl_ref, w_gk2_ref, w_a_ref,
                 w_c_ref, w_out_ref, w_pg_ref, w_pe_ref, g_mix_ref, b_gk_ref, g_gla_ref,
                 conv_w_ref, b_merge_ref, g_ple_ref, g_fin_ref, out_ref, state_ref, carry_ref,
                 apply_final_norm):
    ts = x_ref.shape[1]
    n_chunks = ts // GLA_CHUNK

    x = x_ref[s]
    inv = jnp.broadcast_to(
        lax.rsqrt(jnp.mean(x * x, axis=-1, keepdims=True) + EPS), (ts, D_MODEL))
    h = _bf16(x * g_mix_ref[...])
    yield

    def proj(first_chunk, off, width):
        return inv[:, :width] * jnp.concatenate(
            [jnp.dot(h, w_in_ref[first_chunk + c], preferred_element_type=F32)
             for c in range(off // WCHUNK, (off + width) // WCHUNK)], axis=1)

    proj_a = functools.partial(proj, 0)
    proj_b = functools.partial(proj, COLS_A // WCHUNK)

    gk_low = _bf16(inv[:, :LANE] * jnp.dot(h, w_gkl_ref[...], preferred_element_type=F32))
    q = proj_a(OFF_Q, GLA_KW)
    k = proj_a(OFF_K, GLA_KW)
    yield
    pre = jnp.dot(gk_low, w_gk2_ref[...], preferred_element_type=F32) + b_gk_ref[...]
    v = _bf16(proj_a(OFF_V, GLA_VW))
    z_a = proj_a(OFF_ZA, GLA_VW)
    gk = _log_sigmoid(pre) * (1.0 / GATE_NORMALIZER)
    act_a = _bf16(z_a * _sigmoid(z_a))
    yield
    m = proj_b(OFF_CC, CONV_WIDTH) * proj_b(OFF_XC, CONV_WIDTH)
    cum = gk
    shift = 1
    while shift < ts:
        cum = cum + jnp.concatenate(
            [jnp.zeros((shift, GLA_KW), F32), cum[:ts - shift, :]], axis=0)
        shift *= 2
    row = lax.broadcasted_iota(jnp.int32, (ts, ts), 0)
    col = lax.broadcasted_iota(jnp.int32, (ts, ts), 1)
    causal = col <= row
    yield
    cb = proj_b(OFF_CB, CONV_WIDTH)
    cum_end = cum[ts - 1:ts, :]
    starts = [jnp.zeros((1, GLA_KW), F32)]
    starts += [cum[c * GLA_CHUNK - 1:c * GLA_CHUNK, :] for c in range(1, n_chunks)]
    start_full = jnp.concatenate(
        [jnp.broadcast_to(st, (GLA_CHUNK, GLA_KW)) for st in starts], axis=0)
    exp_start_full = jnp.concatenate(
        [jnp.broadcast_to(jnp.exp(st), (GLA_CHUNK, GLA_KW)) for st in starts], axis=0)
    q_loc = q * (jnp.exp(cum - start_full) * (GLA_DK ** -0.5))
    q_in = _bf16(q_loc)
    q_tile = _bf16(q_loc * exp_start_full)
    k_end = _bf16(k * jnp.exp(cum_end - cum))
    k_rel = []
    for c in range(n_chunks):
        n = _round_up((c + 1) * GLA_CHUNK, LANE)
        k_rel.append(_bf16(k[:n, :] * jnp.exp(starts[c] - cum[:n, :])))
    tile_decay = jnp.exp(cum_end)
    ext = jnp.concatenate([carry_ref[s], m], axis=0)
    carry_ref[s] = m[ts - SUBLANE:, :]
    taps = [m] + [pltpu.roll(ext, j, 0)[SUBLANE:, :] for j in range(1, CONV_K)]
    cw = conv_w_ref[...]
    conv = cw[0:1, :] * taps[CONV_K - 1]
    for j in range(1, CONV_K):
        conv = conv + cw[j:j + 1, :] * taps[CONV_K - 1 - j]
    u = cb * conv
    yield

    attn, s_old = [], []
    for hd in range(GLA_HEADS):
        ks = slice(hd * GLA_DK, (hd + 1) * GLA_DK)
        vs = slice(hd * GLA_DV, (hd + 1) * GLA_DV)
        s_t = state_ref[s, hd]
        s_old.append(_bf16(s_t))
        blocks = []
        for c in range(n_chunks):
            rs = slice(c * GLA_CHUNK, (c + 1) * GLA_CHUNK)
            a_c = lax.dot_general(q_in[rs, ks], k_rel[c][:, ks], _NT,
                                  preferred_element_type=F32)
            if a_c.shape[1] < ts:
                a_c = jnp.concatenate(
                    [a_c, jnp.zeros((GLA_CHUNK, ts - a_c.shape[1]), F32)], axis=1)
            blocks.append(a_c)
        attn.append(_bf16(jnp.where(causal, jnp.concatenate(blocks, axis=0), 0.0)))
        state_ref[s, hd] = s_t * tile_decay[:, ks] + lax.dot_general(
            v[:, vs], k_end[:, ks], _TN, preferred_element_type=F32)
    z_c = proj_b(OFF_ZC, CONV_WIDTH)
    gated_c = _bf16(u * (z_c * _sigmoid(z_c)))
    yield
    g_gla = g_gla_ref[...]
    gated = []
    for hd in range(GLA_HEADS):
        ks = slice(hd * GLA_DK, (hd + 1) * GLA_DK)
        vs = slice(hd * GLA_DV, (hd + 1) * GLA_DV)
        o = (lax.dot_general(q_tile[:, ks], s_old[hd], _NT, preferred_element_type=F32)
             + jnp.dot(attn[hd], v[:, vs], preferred_element_type=F32))
        gated.append(_bf16(_rms(o, g_gla) * act_a[:, vs]))
    b_merge = b_merge_ref[...]
    gate_c = _bf16(_sigmoid(proj_b(OFF_GC, D_MODEL) + b_merge[:, D_MODEL:]))
    yield
    y_c = _bf16(gate_c * jnp.dot(gated_c, w_c_ref[...], preferred_element_type=F32))
    gate_a = _bf16(_sigmoid(proj_b(OFF_GA, D_MODEL) + b_merge[:, :D_MODEL]))
    yield
    y_a = jnp.dot(jnp.concatenate(gated, axis=1), w_a_ref[...], preferred_element_type=F32)
    merged = _bf16(gate_a * y_a + y_c)
    yield

    x1 = x + jnp.dot(merged, w_out_ref[...], preferred_element_type=F32)
    yield
    inv1 = lax.rsqrt(jnp.mean(x1 * x1, axis=-1, keepdims=True) + EPS)
    hn = _bf16(x1 * g_ple_ref[...])
    yield
    ple_gate = _sigmoid(inv1 * jnp.dot(hn, w_pg_ref[...], preferred_element_type=F32))
    x2 = x1 + ple_gate * jnp.dot(_bf16(p_ref[s]), w_pe_ref[...], preferred_element_type=F32)
    yield
    if apply_final_norm:
        x2 = _rms(x2, g_fin_ref[...])
    out_ref[s] = x2


def _layer_kernel(x_ref, p_ref, w_in_t_hbm, wa_hbm, wc_hbm, wo_hbm, wpg_hbm, wpe_hbm,
                  gate_rows_ref, w_gk2_f32_ref, g_mix_ref, b_gk_ref, g_gla_ref, conv_w_ref,
                  b_merge_ref, g_ple_ref, g_fin_ref, out_ref,
                  w_in_ref, w_gkl_ref, w_gk2_ref, w_a_ref, w_c_ref, w_out_ref,
                  w_pg_ref, w_pe_ref, stage_ref, sem_ref, state_ref, carry_ref,
                  *, layer, apply_final_norm):
    @pl.when((pl.program_id(0) == 0) & (pl.program_id(1) == 0))
    def _():
        _load_weights(layer, w_in_t_hbm, (wa_hbm, wc_hbm, wo_hbm, wpg_hbm, wpe_hbm),
                      gate_rows_ref, w_gk2_f32_ref, w_in_ref, w_gkl_ref, w_gk2_ref,
                      (w_a_ref, w_c_ref, w_out_ref, w_pg_ref, w_pe_ref), stage_ref, sem_ref)

    @pl.when(pl.program_id(1) == 0)
    def _():
        state_ref[...] = jnp.zeros_like(state_ref)
        carry_ref[...] = jnp.zeros_like(carry_ref)

    live = [_tile_stages(s, x_ref, p_ref, w_in_ref, w_gkl_ref, w_gk2_ref, w_a_ref,
                         w_c_ref, w_out_ref, w_pg_ref, w_pe_ref, g_mix_ref, b_gk_ref, g_gla_ref,
                         conv_w_ref, b_merge_ref, g_ple_ref, g_fin_ref, out_ref, state_ref,
                         carry_ref, apply_final_norm)
            for s in range(x_ref.shape[0])]
    while live:
        for g in list(live):
            try:
                next(g)
            except StopIteration:
                live.remove(g)


def _const_spec(shape, index=None):
    index = (0,) * len(shape) if index is None else index
    return pl.BlockSpec(shape, lambda b, t: index, pipeline_mode=pl.Buffered(1))


def _layer(layer, x, p_i, w_in, w_gk2, big_weights, vectors, *, apply_final_norm):
    bsz, seq, d = x.shape
    assert d == D_MODEL and seq % SEQ_TILE == 0 and SEQ_TILE % LANE == 0
    assert w_in.shape[1:] == (D_MODEL, IN_COLS) and COLS_A % GATE_RANK == 0
    assert CONV_K - 1 <= SUBLANE
    nb = 2 if bsz % 2 == 0 else 1
    w_in_t = jnp.swapaxes(w_in, 1, 2)
    tile = lambda width: pl.BlockSpec((nb, SEQ_TILE, width), lambda b, t: (b, t, 0))
    in_hbm = pl.BlockSpec(memory_space=pltpu.HBM)
    in_specs = ([tile(D_MODEL), tile(PLE_DIM)] + [in_hbm] * (1 + len(big_weights))
                + [_const_spec((None, GATE_RANK, D_MODEL), (layer, COLS_A // GATE_RANK, 0)),
                   _const_spec((None, GATE_RANK, GLA_KW), (layer, 0, 0))]
                + [_const_spec(vv.shape) if vv.ndim == 2
                   else _const_spec((None,) + vv.shape[1:], (layer, 0, 0)) for vv in vectors])
    bf16 = jnp.bfloat16
    return pl.pallas_call(
        functools.partial(_layer_kernel, layer=layer, apply_final_norm=apply_final_norm),
        grid=(bsz // nb, seq // SEQ_TILE),
        in_specs=in_specs,
        out_specs=tile(D_MODEL),
        out_shape=jax.ShapeDtypeStruct(x.shape, jnp.float32),
        scratch_shapes=[
            pltpu.VMEM(((COLS_A + COLS_B) // WCHUNK, D_MODEL, WCHUNK), bf16),
            pltpu.VMEM((D_MODEL, LANE), bf16),
            pltpu.VMEM((LANE, GLA_KW), bf16),
            pltpu.VMEM((GLA_VW, D_MODEL), bf16),
            pltpu.VMEM((CONV_WIDTH, D_MODEL), bf16),
            pltpu.VMEM((D_MODEL, D_MODEL), bf16),
            pltpu.VMEM((D_MODEL, D_MODEL), bf16),
            pltpu.VMEM((PLE_DIM, D_MODEL), bf16),
            pltpu.VMEM((WSLOTS, WCHUNK, D_MODEL), jnp.float32),
            pltpu.SemaphoreType.DMA((WSLOTS,)),
            pltpu.VMEM((nb, GLA_HEADS, GLA_DV, GLA_DK), jnp.float32),
            pltpu.VMEM((nb, SUBLANE, CONV_WIDTH), jnp.float32),
        ],
        compiler_params=pltpu.CompilerParams(
            dimension_semantics=("arbitrary", "arbitrary"),
            vmem_limit_bytes=VMEM_LIMIT_BYTES),
        name="hybrid_layer",
    )(x, p_i, w_in_t, *big_weights, w_in_t, w_gk2, *vectors)


def kernel(x, p, norm_mix_g, w_in, b_merge, w_gk2, b_gk, gla_norm_g, conv_w, w_branch_a,
           w_branch_c, w_out, norm_ple_g, w_ple_gate, w_ple_proj, norm_final_g):
    depth = w_in.shape[0]
    row = lambda a: a.reshape(1, -1).astype(jnp.float32)
    for i in range(depth):
        vectors = [row(norm_mix_g[i]), row(b_gk[i]), row(gla_norm_g[i]),
                   conv_w.astype(jnp.float32), row(b_merge[i]), row(norm_ple_g[i]),
                   row(norm_final_g)]
        x = _layer(i, x, p[i], w_in, w_gk2,
                   [w_branch_a, w_branch_c, w_out, w_ple_gate, w_ple_proj], vectors,
                   apply_final_norm=(i == depth - 1))
    return x
```

```python
import functools

import jax
import jax.numpy as jnp
from jax import lax
from jax.experimental import pallas as pl
from jax.experimental.pallas import tpu as pltpu

D_MODEL = 1024
GLA_HEADS = 4
GLA_DK = 128
GLA_DV = 256
GLA_KW = GLA_HEADS * GLA_DK
GLA_VW = GLA_HEADS * GLA_DV
GATE_RANK = 16
GATE_NORMALIZER = 16.0
GLA_CHUNK = 64
CONV_WIDTH = 1024
CONV_K = 3
PLE_DIM = 256
EPS = 1e-6

LANE = 128
SUBLANE = 8
SEQ_TILE = 256
WCHUNK = 256
WSLOTS = 4
VMEM_LIMIT_BYTES = 58 * 1024 * 1024

OFF_Q = 0
OFF_K = OFF_Q + GLA_KW
OFF_V = OFF_K + GLA_KW
OFF_ZA = OFF_V + GLA_VW
COLS_A = OFF_ZA + GLA_VW
OFF_CB = 0
OFF_CC = OFF_CB + CONV_WIDTH
OFF_XC = OFF_CC + CONV_WIDTH
OFF_ZC = OFF_XC + CONV_WIDTH
OFF_GA = OFF_ZC + CONV_WIDTH
OFF_GC = OFF_GA + D_MODEL
COLS_B = OFF_GC + D_MODEL
IN_COLS = COLS_A + GATE_RANK + COLS_B

_NT = (((1,), (1,)), ((), ()))
_TN = (((0,), (0,)), ((), ()))
F32 = jnp.float32


def _bf16(a):
    return a.astype(jnp.bfloat16)


def _rms(xf, g):
    ms = jnp.mean(xf * xf, axis=-1, keepdims=True)
    return xf * lax.rsqrt(ms + EPS) * g


def _sigmoid(a):
    return 0.5 * jnp.tanh(0.5 * a) + 0.5


def _log_sigmoid(a):
    return jnp.minimum(a, 0.0) - jnp.log(1.0 + jnp.exp(-jnp.abs(a)))


def _round_up(n, m):
    return (n + m - 1) // m * m


def _load_weights(layer, w_in_t_hbm, dense_hbm, gate_rows_ref, w_gk2_f32_ref,
                  w_in_ref, w_gkl_ref, w_gk2_ref, dense_refs, stage_ref, sem_ref):
    ahead = WSLOTS - 1
    n_in = (COLS_A + COLS_B) // WCHUNK
    dense = [(hbm, dst, r) for hbm, dst in zip(dense_hbm, dense_refs)
             for r in range(0, dst.shape[0], WCHUNK)]
    assert ahead <= len(dense) and ahead < n_in

    def copy(src, g):
        return pltpu.make_async_copy(src, stage_ref.at[g % WSLOTS], sem_ref.at[g % WSLOTS])

    def in_proj_copy(c):
        first = c * WCHUNK + jnp.where(c >= COLS_A // WCHUNK, GATE_RANK, 0)
        return copy(w_in_t_hbm.at[layer, pl.ds(first, WCHUNK), :], c)

    def dense_copy(i):
        hbm, _, r = dense[i]
        return copy(hbm.at[layer, pl.ds(r, WCHUNK), :], n_in + i)

    def in_proj_consume(c):
        w_in_ref[c] = _bf16(stage_ref[c % WSLOTS].T)

    for c in range(ahead):
        in_proj_copy(c).start(priority=c % 2)

    unroll = 3
    assert (n_in - ahead) % unroll == 0

    def body(i, carry):
        for j in range(unroll):
            c = i * unroll + j
            in_proj_copy(c).wait()
            in_proj_consume(c)
            in_proj_copy(c + ahead).start(priority=j % 2)
        return carry

    lax.fori_loop(0, (n_in - ahead) // unroll, body, 0)
    for c in range(n_in - ahead, n_in):
        in_proj_copy(c).wait()
        in_proj_consume(c)
        dense_copy(c + ahead - n_in).start(priority=c % 2)
    for i, (_, dst, r) in enumerate(dense):
        dense_copy(i).wait()
        dst[r:r + WCHUNK, :] = _bf16(stage_ref[(n_in + i) % WSLOTS])
        if i + ahead < len(dense):
            dense_copy(i + ahead).start(priority=i % 2)

    gate_rows = jnp.concatenate(
        [gate_rows_ref[...], jnp.zeros((LANE - GATE_RANK, D_MODEL), F32)], axis=0)
    w_gkl_ref[...] = _bf16(gate_rows.T)
    w_gk2_ref[...] = jnp.zeros_like(w_gk2_ref)
    w_gk2_ref[:GATE_RANK, :] = _bf16(w_gk2_f32_ref[...])


def _tile_stages(s, x_ref, p_ref, w_in_ref, w_gkl_ref, w_gk2_ref, w_a_ref,
                 w_c_ref, w_out_ref, w_pg_ref, w_pe_ref, g_mix_ref, b_gk_ref, g_gla_ref,
                 conv_w_ref, b_merge_ref, g_ple_ref, g_fin_ref, out_ref, state_ref, carry_ref,
                 apply_final_norm):
    ts = x_ref.shape[1]
    n_chunks = ts // GLA_CHUNK

    x = x_ref[s]
    inv = jnp.broadcast_to(
        lax.rsqrt(jnp.mean(x * x, axis=-1, keepdims=True) + EPS), (ts, D_MODEL))
    h = _bf16(x * g_mix_ref[...])
    yield

    def proj(first_chunk, off, width):
        return inv[:, :width] * jnp.concatenate(
            [jnp.dot(h, w_in_ref[first_chunk + c], preferred_element_type=F32)
             for c in range(off // WCHUNK, (off + width) // WCHUNK)], axis=1)

    proj_a = functools.partial(proj, 0)
    proj_b = functools.partial(proj, COLS_A // WCHUNK)

    gk_low = _bf16(inv[:, :LANE] * jnp.dot(h, w_gkl_ref[...], preferred_element_type=F32))
    q = proj_a(OFF_Q, GLA_KW)
    k = proj_a(OFF_K, GLA_KW)
    yield
    pre = jnp.dot(gk_low, w_gk2_ref[...], preferred_element_type=F32) + b_gk_ref[...]
    v = _bf16(proj_a(OFF_V, GLA_VW))
    yield
    z_a = proj_a(OFF_ZA, GLA_VW)
    gk = _log_sigmoid(pre) * (1.0 / GATE_NORMALIZER)
    act_a = _bf16(z_a * _sigmoid(z_a))
    yield
    m = proj_b(OFF_CC, CONV_WIDTH) * proj_b(OFF_XC, CONV_WIDTH)
    cum = gk
    shift = 1
    while shift < ts:
        cum = cum + jnp.concatenate(
            [jnp.zeros((shift, GLA_KW), F32), cum[:ts - shift, :]], axis=0)
        shift *= 2
    row = lax.broadcasted_iota(jnp.int32, (ts, ts), 0)
    col = lax.broadcasted_iota(jnp.int32, (ts, ts), 1)
    causal = col <= row
    yield
    cb = proj_b(OFF_CB, CONV_WIDTH)
    cum_end = cum[ts - 1:ts, :]
    starts = [jnp.zeros((1, GLA_KW), F32)]
    starts += [cum[c * GLA_CHUNK - 1:c * GLA_CHUNK, :] for c in range(1, n_chunks)]
    start_full = jnp.concatenate(
        [jnp.broadcast_to(st, (GLA_CHUNK, GLA_KW)) for st in starts], axis=0)
    exp_start_full = jnp.concatenate(
        [jnp.broadcast_to(jnp.exp(st), (GLA_CHUNK, GLA_KW)) for st in starts], axis=0)
    q_loc = q * (jnp.exp(cum - start_full) * (GLA_DK ** -0.5))
    q_in = _bf16(q_loc)
    q_tile = _bf16(q_loc * exp_start_full)
    k_end = _bf16(k * jnp.exp(cum_end - cum))
    k_rel = []
    for c in range(n_chunks):
        n = _round_up((c + 1) * GLA_CHUNK, LANE)
        k_rel.append(_bf16(k[:n, :] * jnp.exp(starts[c] - cum[:n, :])))
    tile_decay = jnp.exp(cum_end)
    ext = jnp.concatenate([carry_ref[s], m], axis=0)
    carry_ref[s] = m[ts - SUBLANE:, :]
    taps = [m] + [pltpu.roll(ext, j, 0)[SUBLANE:, :] for j in range(1, CONV_K)]
    cw = conv_w_ref[...]
    conv = cw[0:1, :] * taps[CONV_K - 1]
    for j in range(1, CONV_K):
        conv = conv + cw[j:j + 1, :] * taps[CONV_K - 1 - j]
    u = cb * conv
    yield

    attn, s_old = [], []
    for hd in range(GLA_HEADS):
        ks = slice(hd * GLA_DK, (hd + 1) * GLA_DK)
        vs = slice(hd * GLA_DV, (hd + 1) * GLA_DV)
        s_t = state_ref[s, hd]
        s_old.append(_bf16(s_t))
        blocks = []
        for c in range(n_chunks):
            rs = slice(c * GLA_CHUNK, (c + 1) * GLA_CHUNK)
            a_c = lax.dot_general(q_in[rs, ks], k_rel[c][:, ks], _NT,
                                  preferred_element_type=F32)
            if a_c.shape[1] < ts:
                a_c = jnp.concatenate(
                    [a_c, jnp.zeros((GLA_CHUNK, ts - a_c.shape[1]), F32)], axis=1)
            blocks.append(a_c)
        attn.append(_bf16(jnp.where(causal, jnp.concatenate(blocks, axis=0), 0.0)))
        state_ref[s, hd] = s_t * tile_decay[:, ks] + lax.dot_general(
            v[:, vs], k_end[:, ks], _TN, preferred_element_type=F32)
    z_c = proj_b(OFF_ZC, CONV_WIDTH)
    gated_c = _bf16(u * (z_c * _sigmoid(z_c)))
    yield
    g_gla = g_gla_ref[...]
    gated = []
    for hd in range(GLA_HEADS):
        ks = slice(hd * GLA_DK, (hd + 1) * GLA_DK)
        vs = slice(hd * GLA_DV, (hd + 1) * GLA_DV)
        o = (lax.dot_general(q_tile[:, ks], s_old[hd], _NT, preferred_element_type=F32)
             + jnp.dot(attn[hd], v[:, vs], preferred_element_type=F32))
        gated.append(_bf16(_rms(o, g_gla) * act_a[:, vs]))
    b_merge = b_merge_ref[...]
    gate_c = _bf16(_sigmoid(proj_b(OFF_GC, D_MODEL) + b_merge[:, D_MODEL:]))
    yield
    y_c = _bf16(gate_c * jnp.dot(gated_c, w_c_ref[...], preferred_element_type=F32))
    gate_a = _bf16(_sigmoid(proj_b(OFF_GA, D_MODEL) + b_merge[:, :D_MODEL]))
    yield
    y_a = jnp.dot(jnp.concatenate(gated, axis=1), w_a_ref[...], preferred_element_type=F32)
    merged = _bf16(gate_a * y_a + y_c)
    yield

    x1 = x + jnp.dot(merged, w_out_ref[...], preferred_element_type=F32)
    yield
    inv1 = lax.rsqrt(jnp.mean(x1 * x1, axis=-1, keepdims=True) + EPS)
    hn = _bf16(x1 * g_ple_ref[...])
    yield
    ple_gate = _sigmoid(inv1 * jnp.dot(hn, w_pg_ref[...], preferred_element_type=F32))
    x2 = x1 + ple_gate * jnp.dot(_bf16(p_ref[s]), w_pe_ref[...], preferred_element_type=F32)
    yield
    if apply_final_norm:
        x2 = _rms(x2, g_fin_ref[...])
    out_ref[s] = x2


def _layer_kernel(x_ref, p_ref, w_in_t_hbm, wa_hbm, wc_hbm, wo_hbm, wpg_hbm, wpe_hbm,
                  gate_rows_ref, w_gk2_f32_ref, g_mix_ref, b_gk_ref, g_gla_ref, conv_w_ref,
                  b_merge_ref, g_ple_ref, g_fin_ref, out_ref,
                  w_in_ref, w_gkl_ref, w_gk2_ref, w_a_ref, w_c_ref, w_out_ref,
                  w_pg_ref, w_pe_ref, stage_ref, sem_ref, state_ref, carry_ref,
                  *, layer, apply_final_norm):
    @pl.when((pl.program_id(0) == 0) & (pl.program_id(1) == 0))
    def _():
        _load_weights(layer, w_in_t_hbm, (wa_hbm, wc_hbm, wo_hbm, wpg_hbm, wpe_hbm),
                      gate_rows_ref, w_gk2_f32_ref, w_in_ref, w_gkl_ref, w_gk2_ref,
                      (w_a_ref, w_c_ref, w_out_ref, w_pg_ref, w_pe_ref), stage_ref, sem_ref)

    @pl.when(pl.program_id(1) == 0)
    def _():
        state_ref[...] = jnp.zeros_like(state_ref)
        carry_ref[...] = jnp.zeros_like(carry_ref)

    live = [_tile_stages(s, x_ref, p_ref, w_in_ref, w_gkl_ref, w_gk2_ref, w_a_ref,
                         w_c_ref, w_out_ref, w_pg_ref, w_pe_ref, g_mix_ref, b_gk_ref, g_gla_ref,
                         conv_w_ref, b_merge_ref, g_ple_ref, g_fin_ref, out_ref, state_ref,
                         carry_ref, apply_final_norm)
            for s in range(x_ref.shape[0])]
    while live:
        for g in list(live):
            try:
                next(g)
            except StopIteration:
                live.remove(g)


def _const_spec(shape, index=None):
    index = (0,) * len(shape) if index is None else index
    return pl.BlockSpec(shape, lambda b, t: index, pipeline_mode=pl.Buffered(1))


def _layer(layer, x, p_i, w_in, w_gk2, big_weights, vectors, *, apply_final_norm):
    bsz, seq, d = x.shape
    assert d == D_MODEL and seq % SEQ_TILE == 0 and SEQ_TILE % LANE == 0
    assert w_in.shape[1:] == (D_MODEL, IN_COLS) and COLS_A % GATE_RANK == 0
    assert CONV_K - 1 <= SUBLANE
    nb = 2 if bsz % 2 == 0 else 1
    w_in_t = jnp.swapaxes(w_in, 1, 2)
    tile = lambda width: pl.BlockSpec((nb, SEQ_TILE, width), lambda b, t: (b, t, 0))
    in_hbm = pl.BlockSpec(memory_space=pltpu.HBM)
    in_specs = ([tile(D_MODEL), tile(PLE_DIM)] + [in_hbm] * (1 + len(big_weights))
                + [_const_spec((None, GATE_RANK, D_MODEL), (layer, COLS_A // GATE_RANK, 0)),
                   _const_spec((None, GATE_RANK, GLA_KW), (layer, 0, 0))]
                + [_const_spec(vv.shape) if vv.ndim == 2
                   else _const_spec((None,) + vv.shape[1:], (layer, 0, 0)) for vv in vectors])
    bf16 = jnp.bfloat16
    return pl.pallas_call(
        functools.partial(_layer_kernel, layer=layer, apply_final_norm=apply_final_norm),
        grid=(bsz // nb, seq // SEQ_TILE),
        in_specs=in_specs,
        out_specs=tile(D_MODEL),
        out_shape=jax.ShapeDtypeStruct(x.shape, jnp.float32),
        scratch_shapes=[
            pltpu.VMEM(((COLS_A + COLS_B) // WCHUNK, D_MODEL, WCHUNK), bf16),
            pltpu.VMEM((D_MODEL, LANE), bf16),
            pltpu.VMEM((LANE, GLA_KW), bf16),
            pltpu.VMEM((GLA_VW, D_MODEL), bf16),
            pltpu.VMEM((CONV_WIDTH, D_MODEL), bf16),
            pltpu.VMEM((D_MODEL, D_MODEL), bf16),
            pltpu.VMEM((D_MODEL, D_MODEL), bf16),
            pltpu.VMEM((PLE_DIM, D_MODEL), bf16),
            pltpu.VMEM((WSLOTS, WCHUNK, D_MODEL), jnp.float32),
            pltpu.SemaphoreType.DMA((WSLOTS,)),
            pltpu.VMEM((nb, GLA_HEADS, GLA_DV, GLA_DK), jnp.float32),
            pltpu.VMEM((nb, SUBLANE, CONV_WIDTH), jnp.float32),
        ],
        compiler_params=pltpu.CompilerParams(
            dimension_semantics=("arbitrary", "arbitrary"),
            vmem_limit_bytes=VMEM_LIMIT_BYTES),
        name="hybrid_layer",
    )(x, p_i, w_in_t, *big_weights, w_in_t, w_gk2, *vectors)


def kernel(x, p, norm_mix_g, w_in, b_merge, w_gk2, b_gk, gla_norm_g, conv_w, w_branch_a,
           w_branch_c, w_out, norm_ple_g, w_ple_gate, w_ple_proj, norm_final_g):
    depth = w_in.shape[0]
    row = lambda a: a.reshape(1, -1).astype(jnp.float32)
    for i in range(depth):
        vectors = [row(norm_mix_g[i]), row(b_gk[i]), row(gla_norm_g[i]),
                   conv_w.astype(jnp.float32), row(b_merge[i]), row(norm_ple_g[i]),
                   row(norm_final_g)]
        x = _layer(i, x, p[i], w_in, w_gk2,
                   [w_branch_a, w_branch_c, w_out, w_ple_gate, w_ple_proj], vectors,
                   apply_final_norm=(i == depth - 1))
    return x
```

```python
import functools

import jax
import jax.numpy as jnp
from jax import lax
from jax.experimental import pallas as pl
from jax.experimental.pallas import tpu as pltpu

D_MODEL = 1024
GLA_HEADS = 4
GLA_DK = 128
GLA_DV = 256
GLA_KW = GLA_HEADS * GLA_DK
GLA_VW = GLA_HEADS * GLA_DV
GATE_RANK = 16
GATE_NORMALIZER = 16.0
GLA_CHUNK = 64
CONV_WIDTH = 1024
CONV_K = 3
PLE_DIM = 256
EPS = 1e-6

LANE = 128
SUBLANE = 8
SEQ_TILE = 256
WCHUNK = 256
WSLOTS = 4
VMEM_LIMIT_BYTES = 58 * 1024 * 1024

OFF_Q = 0
OFF_K = OFF_Q + GLA_KW
OFF_V = OFF_K + GLA_KW
OFF_ZA = OFF_V + GLA_VW
COLS_A = OFF_ZA + GLA_VW
OFF_CB = 0
OFF_CC = OFF_CB + CONV_WIDTH
OFF_XC = OFF_CC + CONV_WIDTH
OFF_ZC = OFF_XC + CONV_WIDTH
OFF_GA = OFF_ZC + CONV_WIDTH
OFF_GC = OFF_GA + D_MODEL
COLS_B = OFF_GC + D_MODEL
IN_COLS = COLS_A + GATE_RANK + COLS_B

_NT = (((1,), (1,)), ((), ()))
_TN = (((0,), (0,)), ((), ()))
F32 = jnp.float32


def _bf16(a):
    return a.astype(jnp.bfloat16)


def _rms(xf, g):
    ms = jnp.mean(xf * xf, axis=-1, keepdims=True)
    return xf * lax.rsqrt(ms + EPS) * g


def _sigmoid(a):
    return 0.5 * jnp.tanh(0.5 * a) + 0.5


def _log_sigmoid(a):
    return jnp.minimum(a, 0.0) - jnp.log(1.0 + jnp.exp(-jnp.abs(a)))


def _round_up(n, m):
    return (n + m - 1) // m * m


def _load_weights(layer, w_in_t_hbm, dense_hbm, gate_rows_ref, w_gk2_f32_ref,
                  w_in_ref, w_gkl_ref, w_gk2_ref, dense_refs, stage_ref, sem_ref):
    ahead = WSLOTS - 1
    n_in = (COLS_A + COLS_B) // WCHUNK
    dense = [(hbm, dst, r) for hbm, dst in zip(dense_hbm, dense_refs)
             for r in range(0, dst.shape[0], WCHUNK)]
    assert ahead <= len(dense) and ahead < n_in

    def copy(src, g):
        return pltpu.make_async_copy(src, stage_ref.at[g % WSLOTS], sem_ref.at[g % WSLOTS])

    def in_proj_copy(c):
        first = c * WCHUNK + jnp.where(c >= COLS_A // WCHUNK, GATE_RANK, 0)
        return copy(w_in_t_hbm.at[layer, pl.ds(first, WCHUNK), :], c)

    def dense_copy(i):
        hbm, _, r = dense[i]
        return copy(hbm.at[layer, pl.ds(r, WCHUNK), :], n_in + i)

    def in_proj_consume(c):
        w_in_ref[c] = _bf16(stage_ref[c % WSLOTS].T)

    for c in range(ahead):
        in_proj_copy(c).start(priority=c % 2)

    gate_rows = jnp.concatenate(
        [gate_rows_ref[...], jnp.zeros((LANE - GATE_RANK, D_MODEL), F32)], axis=0)
    w_gkl_ref[...] = _bf16(gate_rows.T)
    w_gk2_ref[...] = jnp.zeros_like(w_gk2_ref)
    w_gk2_ref[:GATE_RANK, :] = _bf16(w_gk2_f32_ref[...])

    unroll = 3
    assert (n_in - ahead) % unroll == 0

    def body(i, carry):
        for j in range(unroll):
            c = i * unroll + j
            in_proj_copy(c).wait()
            in_proj_copy(c + ahead).start(priority=j % 2)
            in_proj_consume(c)
        return carry

    lax.fori_loop(0, (n_in - ahead) // unroll, body, 0)
    for c in range(n_in - ahead, n_in):
        in_proj_copy(c).wait()
        dense_copy(c + ahead - n_in).start(priority=c % 2)
        in_proj_consume(c)
    for i, (_, dst, r) in enumerate(dense):
        dense_copy(i).wait()
        if i + ahead < len(dense):
            dense_copy(i + ahead).start(priority=i % 2)
        dst[r:r + WCHUNK, :] = _bf16(stage_ref[(n_in + i) % WSLOTS])


def _tile_stages(s, x_ref, p_ref, w_in_ref, w_gkl_ref, w_gk2_ref, w_a_ref,
                 w_c_ref, w_out_ref, w_pg_ref, w_pe_ref, g_mix_ref, b_gk_ref, g_gla_ref,
                 conv_w_ref, b_merge_ref, g_ple_ref, g_fin_ref, out_ref, state_ref, carry_ref,
                 apply_final_norm):
    ts = x_ref.shape[1]
    n_chunks = ts // GLA_CHUNK

    x = x_ref[s]
    inv = jnp.broadcast_to(
        lax.rsqrt(jnp.mean(x * x, axis=-1, keepdims=True) + EPS), (ts, D_MODEL))
    h = _bf16(x * g_mix_ref[...])
    yield

    def proj(first_chunk, off, width):
        return inv[:, :width] * jnp.concatenate(
            [jnp.dot(h, w_in_ref[first_chunk + c], preferred_element_type=F32)
             for c in range(off // WCHUNK, (off + width) // WCHUNK)], axis=1)

    proj_a = functools.partial(proj, 0)
    proj_b = functools.partial(proj, COLS_A // WCHUNK)

    gk_low = _bf16(inv[:, :LANE] * jnp.dot(h, w_gkl_ref[...], preferred_element_type=F32))
    q = proj_a(OFF_Q, GLA_KW)
    k = proj_a(OFF_K, GLA_KW)
    yield
    pre = jnp.dot(gk_low, w_gk2_ref[...], preferred_element_type=F32) + b_gk_ref[...]
    v = _bf16(proj_a(OFF_V, GLA_VW))
    yield
    z_a = proj_a(OFF_ZA, GLA_VW)
    gk = _log_sigmoid(pre) * (1.0 / GATE_NORMALIZER)
    act_a = _bf16(z_a * _sigmoid(z_a))
    yield
    m = proj_b(OFF_CC, CONV_WIDTH) * proj_b(OFF_XC, CONV_WIDTH)
    cum = gk
    shift = 1
    while shift < ts:
        cum = cum + jnp.concatenate(
            [jnp.zeros((shift, GLA_KW), F32), cum[:ts - shift, :]], axis=0)
        shift *= 2
    row = lax.broadcasted_iota(jnp.int32, (ts, ts), 0)
    col = lax.broadcasted_iota(jnp.int32, (ts, ts), 1)
    causal = col <= row
    yield
    cb = proj_b(OFF_CB, CONV_WIDTH)
    cum_end = cum[ts - 1:ts, :]
    starts = [jnp.zeros((1, GLA_KW), F32)]
    starts += [cum[c * GLA_CHUNK - 1:c * GLA_CHUNK, :] for c in range(1, n_chunks)]
    start_full = jnp.concatenate(
        [jnp.broadcast_to(st, (GLA_CHUNK, GLA_KW)) for st in starts], axis=0)
    exp_start_full = jnp.concatenate(
        [jnp.broadcast_to(jnp.exp(st), (GLA_CHUNK, GLA_KW)) for st in starts], axis=0)
    q_loc = q * (jnp.exp(cum - start_full) * (GLA_DK ** -0.5))
    q_in = _bf16(q_loc)
    q_tile = _bf16(q_loc * exp_start_full)
    k_end = _bf16(k * jnp.exp(cum_end - cum))
    k_rel = []
    for c in range(n_chunks):
        n = _round_up((c + 1) * GLA_CHUNK, LANE)
        k_rel.append(_bf16(k[:n, :] * jnp.exp(starts[c] - cum[:n, :])))
    tile_decay = jnp.exp(cum_end)
    ext = jnp.concatenate([carry_ref[s], m], axis=0)
    carry_ref[s] = m[ts - SUBLANE:, :]
    taps = [m] + [pltpu.roll(ext, j, 0)[SUBLANE:, :] for j in range(1, CONV_K)]
    cw = conv_w_ref[...]
    conv = cw[0:1, :] * taps[CONV_K - 1]
    for j in range(1, CONV_K):
        conv = conv + cw[j:j + 1, :] * taps[CONV_K - 1 - j]
    u = cb * conv
    yield

    attn, s_old = [], []
    for hd in range(GLA_HEADS):
        ks = slice(hd * GLA_DK, (hd + 1) * GLA_DK)
        vs = slice(hd * GLA_DV, (hd + 1) * GLA_DV)
        s_t = state_ref[s, hd]
        s_old.append(_bf16(s_t))
        blocks = []
        for c in range(n_chunks):
            rs = slice(c * GLA_CHUNK, (c + 1) * GLA_CHUNK)
            a_c = lax.dot_general(q_in[rs, ks], k_rel[c][:, ks], _NT,
                                  preferred_element_type=F32)
            if a_c.shape[1] < ts:
                a_c = jnp.concatenate(
                    [a_c, jnp.zeros((GLA_CHUNK, ts - a_c.shape[1]), F32)], axis=1)
            blocks.append(a_c)
        attn.append(_bf16(jnp.where(causal, jnp.concatenate(blocks, axis=0), 0.0)))
        state_ref[s, hd] = s_t * tile_decay[:, ks] + lax.dot_general(
            v[:, vs], k_end[:, ks], _TN, preferred_element_type=F32)
    z_c = proj_b(OFF_ZC, CONV_WIDTH)
    gated_c = _bf16(u * (z_c * _sigmoid(z_c)))
    yield
    g_gla = g_gla_ref[...]
    gated = []
    for hd in range(GLA_HEADS):
        ks = slice(hd * GLA_DK, (hd + 1) * GLA_DK)
        vs = slice(hd * GLA_DV, (hd + 1) * GLA_DV)
        o = (lax.dot_general(q_tile[:, ks], s_old[hd], _NT, preferred_element_type=F32)
             + jnp.dot(attn[hd], v[:, vs], preferred_element_type=F32))
        gated.append(_bf16(_rms(o, g_gla) * act_a[:, vs]))
    b_merge = b_merge_ref[...]
    gate_c = _bf16(_sigmoid(proj_b(OFF_GC, D_MODEL) + b_merge[:, D_MODEL:]))
    yield
    y_c = _bf16(gate_c * jnp.dot(gated_c, w_c_ref[...], preferred_element_type=F32))
    gate_a = _bf16(_sigmoid(proj_b(OFF_GA, D_MODEL) + b_merge[:, :D_MODEL]))
    yield
    y_a = jnp.dot(jnp.concatenate(gated, axis=1), w_a_ref[...], preferred_element_type=F32)
    merged = _bf16(gate_a * y_a + y_c)
    yield

    x1 = x + jnp.dot(merged, w_out_ref[...], preferred_element_type=F32)
    yield
    inv1 = lax.rsqrt(jnp.mean(x1 * x1, axis=-1, keepdims=True) + EPS)
    hn = _bf16(x1 * g_ple_ref[...])
    yield
    ple_gate = _sigmoid(inv1 * jnp.dot(hn, w_pg_ref[...], preferred_element_type=F32))
    x2 = x1 + ple_gate * jnp.dot(_bf16(p_ref[s]), w_pe_ref[...], preferred_element_type=F32)
    yield
    if apply_final_norm:
        x2 = _rms(x2, g_fin_ref[...])
    out_ref[s] = x2


def _layer_kernel(x_ref, p_ref, w_in_t_hbm, wa_hbm, wc_hbm, wo_hbm, wpg_hbm, wpe_hbm,
                  gate_rows_ref, w_gk2_f32_ref, g_mix_ref, b_gk_ref, g_gla_ref, conv_w_ref,
                  b_merge_ref, g_ple_ref, g_fin_ref, out_ref,
                  w_in_ref, w_gkl_ref, w_gk2_ref, w_a_ref, w_c_ref, w_out_ref,
                  w_pg_ref, w_pe_ref, stage_ref, sem_ref, state_ref, carry_ref,
                  *, layer, apply_final_norm):
    @pl.when((pl.program_id(0) == 0) & (pl.program_id(1) == 0))
    def _():
        _load_weights(layer, w_in_t_hbm, (wa_hbm, wc_hbm, wo_hbm, wpg_hbm, wpe_hbm),
                      gate_rows_ref, w_gk2_f32_ref, w_in_ref, w_gkl_ref, w_gk2_ref,
                      (w_a_ref, w_c_ref, w_out_ref, w_pg_ref, w_pe_ref), stage_ref, sem_ref)

    @pl.when(pl.program_id(1) == 0)
    def _():
        state_ref[...] = jnp.zeros_like(state_ref)
        carry_ref[...] = jnp.zeros_like(carry_ref)

    live = [_tile_stages(s, x_ref, p_ref, w_in_ref, w_gkl_ref, w_gk2_ref, w_a_ref,
                         w_c_ref, w_out_ref, w_pg_ref, w_pe_ref, g_mix_ref, b_gk_ref, g_gla_ref,
                         conv_w_ref, b_merge_ref, g_ple_ref, g_fin_ref, out_ref, state_ref,
                         carry_ref, apply_final_norm)
            for s in range(x_ref.shape[0])]
    while live:
        for g in list(live):
            try:
                next(g)
            except StopIteration:
                live.remove(g)


def _const_spec(shape, index=None):
    index = (0,) * len(shape) if index is None else index
    return pl.BlockSpec(shape, lambda b, t: index, pipeline_mode=pl.Buffered(1))


def _layer(layer, x, p_i, w_in, w_gk2, big_weights, vectors, *, apply_final_norm):
    bsz, seq, d = x.shape
    assert d == D_MODEL and seq % SEQ_TILE == 0 and SEQ_TILE % LANE == 0
    assert w_in.shape[1:] == (D_MODEL, IN_COLS) and COLS_A % GATE_RANK == 0
    assert CONV_K - 1 <= SUBLANE
    nb = 2 if bsz % 2 == 0 else 1
    w_in_t = jnp.swapaxes(w_in, 1, 2)
    tile = lambda width: pl.BlockSpec((nb, SEQ_TILE, width), lambda b, t: (b, t, 0))
    in_hbm = pl.BlockSpec(memory_space=pltpu.HBM)
    in_specs = ([tile(D_MODEL), tile(PLE_DIM)] + [in_hbm] * (1 + len(big_weights))
                + [_const_spec((None, GATE_RANK, D_MODEL), (layer, COLS_A // GATE_RANK, 0)),
                   _const_spec((None, GATE_RANK, GLA_KW), (layer, 0, 0))]
                + [_const_spec(vv.shape) if vv.ndim == 2
                   else _const_spec((None,) + vv.shape[1:], (layer, 0, 0)) for vv in vectors])
    bf16 = jnp.bfloat16
    return pl.pallas_call(
        functools.partial(_layer_kernel, layer=layer, apply_final_norm=apply_final_norm),
        grid=(bsz // nb, seq // SEQ_TILE),
        in_specs=in_specs,
        out_specs=tile(D_MODEL),
        out_shape=jax.ShapeDtypeStruct(x.shape, jnp.float32),
        scratch_shapes=[
            pltpu.VMEM(((COLS_A + COLS_B) // WCHUNK, D_MODEL, WCHUNK), bf16),
            pltpu.VMEM((D_MODEL, LANE), bf16),
            pltpu.VMEM((LANE, GLA_KW), bf16),
            pltpu.VMEM((GLA_VW, D_MODEL), bf16),
            pltpu.VMEM((CONV_WIDTH, D_MODEL), bf16),
            pltpu.VMEM((D_MODEL, D_MODEL), bf16),
            pltpu.VMEM((D_MODEL, D_MODEL), bf16),
            pltpu.VMEM((PLE_DIM, D_MODEL), bf16),
            pltpu.VMEM((WSLOTS, WCHUNK, D_MODEL), jnp.float32),
            pltpu.SemaphoreType.DMA((WSLOTS,)),
            pltpu.VMEM((nb, GLA_HEADS, GLA_DV, GLA_DK), jnp.float32),
            pltpu.VMEM((nb, SUBLANE, CONV_WIDTH), jnp.float32),
        ],
        compiler_params=pltpu.CompilerParams(
            dimension_semantics=("arbitrary", "arbitrary"),
            vmem_limit_bytes=VMEM_LIMIT_BYTES),
        name="hybrid_layer",
    )(x, p_i, w_in_t, *big_weights, w_in_t, w_gk2, *vectors)


def kernel(x, p, norm_mix_g, w_in, b_merge, w_gk2, b_gk, gla_norm_g, conv_w, w_branch_a,
           w_branch_c, w_out, norm_ple_g, w_ple_gate, w_ple_proj, norm_final_g):
    depth = w_in.shape[0]
    row = lambda a: a.reshape(1, -1).astype(jnp.float32)
    for i in range(depth):
        vectors = [row(norm_mix_g[i]), row(b_gk[i]), row(gla_norm_g[i]),
                   conv_w.astype(jnp.float32), row(b_merge[i]), row(norm_ple_g[i]),
                   row(norm_final_g)]
        x = _layer(i, x, p[i], w_in, w_gk2,
                   [w_branch_a, w_branch_c, w_out, w_ple_gate, w_ple_proj], vectors,
                   apply_final_norm=(i == depth - 1))
    return x
```
